```python
import jax, jax.numpy as jnp
from jax import lax
import numpy as np

D_MODEL = 1024
BATCH = 16
SEQ = 2048
DEPTH = 1

LRU_WIDTH = D_MODEL
LRU_BLOCKS = 16
LRU_BW = LRU_WIDTH // LRU_BLOCKS
CONV_WIDTH = 4
LRU_C = 8.0
RET_HEADS = 4
RET_DK = 256
RET_DV = 256
RET_QK_WIDTH = RET_HEADS * RET_DK
RET_WIDTH = RET_HEADS * RET_DV
CHUNK = 128
ROPE_THETA = 10000.0
EPS = 1e-6
IN_SIZES = (LRU_WIDTH, LRU_WIDTH, RET_QK_WIDTH, RET_QK_WIDTH, RET_WIDTH, RET_WIDTH, D_MODEL, D_MODEL)
IN_COLS = sum(IN_SIZES)
SPLIT_POINTS = tuple(int(c) for c in np.cumsum(IN_SIZES)[:-1])

kernel_name = "hybrid_rglru_retention_gated_block"


def rmsnorm(x, g):
    xf = x.astype(jnp.float32)
    y = xf * lax.rsqrt(jnp.mean(xf * xf, axis=-1, keepdims=True) + EPS)
    return (y * g.astype(jnp.float32)).astype(x.dtype)


def causal_depthwise_conv(x, w, b):
    S = x.shape[1]
    xp = jnp.pad(x, ((0, 0), (CONV_WIDTH - 1, 0), (0, 0)))
    y = b
    for k in range(CONV_WIDTH):
        y = y + xp[:, k:k + S, :] * w[k]
    return y


def block_diag_linear(x, w, b):
    B, S, W = x.shape
    xb = x.reshape(B, S, LRU_BLOCKS, LRU_BW)
    return jnp.einsum('bsnk,nkj->bsnj', xb, w).reshape(B, S, W) + b


def rg_lru(x, wx, bx, wa, ba, lam):
    B, S, W = x.shape
    i_t = jax.nn.sigmoid(block_diag_linear(x, wx, bx))
    r_t = jax.nn.sigmoid(block_diag_linear(x, wa, ba))
    log_a = -LRU_C * r_t.astype(jnp.float32) * jax.nn.softplus(-lam.astype(jnp.float32))
    a = jnp.exp(log_a)
    mult = jnp.sqrt(-jnp.expm1(2.0 * log_a))
    u = mult * (i_t * x).astype(jnp.float32)

    def step(h, au):
        a_t, u_t = au
        h = a_t * h + u_t
        return h, h

    _, hs = lax.scan(step, jnp.zeros((B, W), jnp.float32),
                     (jnp.swapaxes(a, 0, 1), jnp.swapaxes(u, 0, 1)))
    return jnp.swapaxes(hs, 0, 1).astype(x.dtype)


def rotary(x):
    S, D = x.shape[1], x.shape[3]
    half = D // 2
    freqs = ROPE_THETA ** (-jnp.arange(half, dtype=jnp.float32) / half)
    ang = jnp.arange(S, dtype=jnp.float32)[:, None] * freqs[None, :]
    cos = jnp.cos(ang)[None, :, None, :]
    sin = jnp.sin(ang)[None, :, None, :]
    x1, x2 = x[..., :half], x[..., half:]
    return jnp.concatenate([x1 * cos - x2 * sin, x1 * sin + x2 * cos], axis=-1)


def retention_chunkwise(q, k, v):
    B, S, H, DK = q.shape
    DV = v.shape[-1]
    NC = S // CHUNK
    log_g = jnp.log1p(-(2.0 ** (-5.0 - jnp.arange(H, dtype=jnp.float32))))
    idx = jnp.arange(CHUNK, dtype=jnp.float32)
    diff = idx[:, None] - idx[None, :]
    inner_decay = jnp.where(diff >= 0, jnp.exp(jnp.maximum(diff, 0.0)[None] * log_g[:, None, None]), 0.0)
    cross_decay = jnp.exp((idx[:, None] + 1.0) * log_g[None, :])[None, :, :, None]
    state_decay = jnp.exp((CHUNK - 1.0 - idx[:, None]) * log_g[None, :])[None, :, :, None]
    chunk_decay = jnp.exp(CHUNK * log_g)[None, :, None, None]

    def to_chunks(t):
        return jnp.swapaxes(t.reshape(B, NC, CHUNK, H, t.shape[-1]), 0, 1)

    def step(R, qkv):
        qc, kc, vc = qkv
        scores = jnp.einsum('bihd,bjhd->bhij', qc, kc) * inner_decay
        inner = jnp.einsum('bhij,bjhe->bihe', scores, vc)
        cross = jnp.einsum('bihd,bhde->bihe', qc, R) * cross_decay
        R_new = chunk_decay * R + jnp.einsum('bjhd,bjhe->bhde', kc, vc * state_decay)
        return R_new, inner + cross

    R0 = jnp.zeros((B, H, DK, DV), jnp.float32)
    _, out = lax.scan(step, R0, (to_chunks(q), to_chunks(k), to_chunks(v)))
    return jnp.swapaxes(out, 0, 1).reshape(B, S, H, DV)


def head_groupnorm(y, g):
    mu = jnp.mean(y, axis=-1, keepdims=True)
    yc = y - mu
    var = jnp.mean(yc * yc, axis=-1, keepdims=True)
    return yc * lax.rsqrt(var + EPS) * g.astype(jnp.float32)


def setup_inputs(seed: int = 0) -> dict:
    key = jax.random.key(seed)
    ks = jax.random.split(key, 16)
    L = DEPTH
    f32 = jnp.float32
    nrm = lambda k, shape, fan: jax.random.normal(k, shape, f32) * (fan ** -0.5)
    a0 = jax.random.uniform(ks[10], (L, LRU_WIDTH), f32, 0.9, 0.999)
    return {
        "x": jax.random.normal(ks[0], (BATCH, SEQ, D_MODEL), f32),
        "norm_in": 1.0 + 0.02 * jax.random.normal(ks[1], (L, D_MODEL), f32),
        "w_in": nrm(ks[2], (L, D_MODEL, IN_COLS), D_MODEL),
        "conv_w": nrm(ks[3], (L, CONV_WIDTH, LRU_WIDTH), CONV_WIDTH),
        "conv_b": 0.02 * jax.random.normal(ks[4], (L, LRU_WIDTH), f32),
        "gate_x_w": nrm(ks[5], (L, LRU_BLOCKS, LRU_BW, LRU_BW), LRU_BW),
        "gate_x_b": 0.02 * jax.random.normal(ks[6], (L, LRU_WIDTH), f32),
        "gate_a_w": nrm(ks[7], (L, LRU_BLOCKS, LRU_BW, LRU_BW), LRU_BW),
        "gate_a_b": 0.02 * jax.random.normal(ks[8], (L, LRU_WIDTH), f32),
        "lru_lambda": jnp.log(a0 / (1.0 - a0)),
        "gn_gain": 1.0 + 0.02 * jax.random.normal(ks[9], (L, RET_HEADS, RET_DV), f32),
        "w_proj_a": nrm(ks[11], (L, LRU_WIDTH, D_MODEL), LRU_WIDTH),
        "w_proj_b": nrm(ks[12], (L, RET_WIDTH, D_MODEL), RET_WIDTH),
        "w_out": nrm(ks[13], (L, D_MODEL, D_MODEL), D_MODEL),
        "norm_final": 1.0 + 0.02 * jax.random.normal(ks[14], (D_MODEL,), f32),
    }


def reference(x, norm_in, w_in, conv_w, conv_b, gate_x_w, gate_x_b, gate_a_w, gate_a_b,
              lru_lambda, gn_gain, w_proj_a, w_proj_b, w_out, norm_final):
    B, S, _ = x.shape
    for l in range(DEPTH):
        h = rmsnorm(x, norm_in[l])
        proj = jnp.einsum('bsd,dc->bsc', h, w_in[l])
        xa, ga, q, k, v, gb, ma, mb = jnp.split(proj, SPLIT_POINTS, axis=-1)

        xa = causal_depthwise_conv(xa, conv_w[l], conv_b[l])
        ya = rg_lru(xa, gate_x_w[l], gate_x_b[l], gate_a_w[l], gate_a_b[l], lru_lambda[l])
        ya = jax.nn.silu(ga) * ya
        out_a = jnp.einsum('bsw,wd->bsd', ya, w_proj_a[l])

        qh = rotary(q.reshape(B, S, RET_HEADS, RET_DK).astype(jnp.float32))
        kh = rotary(k.reshape(B, S, RET_HEADS, RET_DK).astype(jnp.float32)) * (RET_DK ** -0.5)
        vh = v.reshape(B, S, RET_HEADS, RET_DV).astype(jnp.float32)
        ret = head_groupnorm(retention_chunkwise(qh, kh, vh), gn_gain[l])
        yb = jax.nn.silu(gb) * ret.reshape(B, S, RET_WIDTH).astype(x.dtype)
        out_b = jnp.einsum('bsw,wd->bsd', yb, w_proj_b[l])

        merged = jax.nn.sigmoid(ma) * out_a + jax.nn.sigmoid(mb) * out_b
        x = x + jnp.einsum('bsd,de->bse', merged, w_out[l])
    return rmsnorm(x, norm_final)
```

```python
import functools

import jax
import jax.numpy as jnp
from jax import lax
from jax.experimental import pallas as pl
from jax.experimental.pallas import tpu as pltpu

D_MODEL = 1024
LRU_BLOCKS = 16
LRU_BW = D_MODEL // LRU_BLOCKS
CONV_WIDTH = 4
LRU_C = 8.0
RET_HEADS = 4
RET_DK = 256
RET_DV = 256
CHUNK = 128
ROPE_THETA = 10000.0
EPS = 1e-6

KIND_XA, KIND_GA, KIND_Q, KIND_K, KIND_V, KIND_GB, KIND_MA, KIND_MB = range(8)
N_KINDS = 8
HALVED_KINDS = (KIND_GA, KIND_GB, KIND_MA, KIND_MB)

LANES = 128
SUBLANES = 8
GROUP = 256
N_GROUPS = D_MODEL // GROUP
TILE_T = 256
N_CHUNKS = TILE_T // CHUNK
ROW_CHUNK = 32
PREP_ROWS = 64
HALO = SUBLANES
VMEM_LIMIT_BYTES = 56 * 1024 * 1024
TINY = 1e-37


def _dot(a, b):
    return jnp.dot(a, b, preferred_element_type=jnp.float32)


def _prepare_weights(win_hbm, wpa_hbm, wpb_hbm, wout_hbm, win_s, wpa_s, wpb_s, wout_s,
                     stage, sem):
    slabs = [(win_hbm.at[:, pl.ds(i * D_MODEL, D_MODEL)], i) for i in range(N_KINDS)]
    slabs += [(wpa_hbm, "pa"), (wpb_hbm, "pb"), (wout_hbm, "out")]

    def copy(n):
        return pltpu.make_async_copy(slabs[n][0], stage.at[n % 2], sem.at[n % 2])

    copy(0).start()
    for n, (_, tag) in enumerate(slabs):
        if n + 1 < len(slabs):
            copy(n + 1).start()
        copy(n).wait()
        slot = n % 2
        halve = tag in HALVED_KINDS or tag in ("pa", "pb")

        def convert(i, carry, slot=slot, tag=tag, halve=halve):
            rows = pl.ds(pl.multiple_of(i * PREP_ROWS, PREP_ROWS), PREP_ROWS)
            w = stage[slot, rows, :]
            if halve:
                w = 0.5 * w
            w = w.astype(jnp.bfloat16)
            if isinstance(tag, int):
                for g in range(N_GROUPS):
                    c0 = (g * N_KINDS + tag) * GROUP
                    win_s[rows, c0:c0 + GROUP] = w[:, g * GROUP:(g + 1) * GROUP]
            else:
                {"pa": wpa_s, "pb": wpb_s, "out": wout_s}[tag][rows, :] = w
            return carry

        lax.fori_loop(0, D_MODEL // PREP_ROWS, convert, 0)


def _block_kernel(x_ref, nin_ref, win_hbm, cw_ref, cb_ref, gw_ref, gxb_ref, gab_ref, lam_ref,
                  gn_ref, wpa_hbm, wpb_hbm, wout_hbm, nfin_ref, cos_ref, sin_ref, idec_ref,
                  cdec_ref, sdec_ref, chdec_ref, out_ref,
                  win_s, wpa_s, wpb_s, wout_s, stage, sem,
                  h_ref, xa_ref, lru_ref, ya_ref, yb_ref, tma_ref, tmb_ref, r_ref, hst_ref,
                  *, final_norm):
    t = pl.program_id(1)
    T = TILE_T
    bf16 = jnp.bfloat16

    @pl.when((pl.program_id(0) == 0) & (t == 0))
    def _():
        _prepare_weights(win_hbm, wpa_hbm, wpb_hbm, wout_hbm, win_s, wpa_s, wpb_s, wout_s,
                         stage, sem)

    @pl.when(t == 0)
    def _():
        r_ref[...] = jnp.zeros_like(r_ref)
        hst_ref[...] = jnp.zeros_like(hst_ref)
        xa_ref[:, 0:HALO, :] = jnp.zeros((D_MODEL // LANES, HALO, LANES), jnp.float32)

    g_in = nin_ref[...]
    for r0 in range(0, T, ROW_CHUNK):
        xr = x_ref[0, r0:r0 + ROW_CHUNK, :]
        ms = jnp.mean(xr * xr, axis=-1, keepdims=True)
        h_ref[r0:r0 + ROW_CHUNK, :] = (xr * lax.rsqrt(ms + EPS) * g_in).astype(bf16)

    n_rg = T // SUBLANES
    sub = lax.broadcasted_iota(jnp.int32, (n_rg, SUBLANES, GROUP), 1)
    half = RET_DK // 2
    k_scale = RET_DK ** -0.5

    def project(g):
        w0 = g * N_KINDS * GROUP
        return _dot(h_ref[...], win_s[:, w0:w0 + N_KINDS * GROUP])

    proj = project(0)
    for g in range(N_GROUPS):
        cs = slice(g * GROUP, (g + 1) * GROUP)
        kind = lambda i, p=proj: p[:, i * GROUP:(i + 1) * GROUP]
        tma_ref[:, cs] = 1.0 + jnp.tanh(kind(KIND_MA))
        tmb_ref[:, cs] = 1.0 + jnp.tanh(kind(KIND_MB))

        xa = kind(KIND_XA)
        xc_blocks = []
        for b in range(GROUP // LANES):
            blk = g * (GROUP // LANES) + b
            lc = slice(blk * LANES, (blk + 1) * LANES)
            xa_b = xa[:, b * LANES:(b + 1) * LANES]
            xa_ref[blk, HALO:HALO + T, :] = xa_b
            xc_b = cb_ref[:, lc] + cw_ref[CONV_WIDTH - 1:CONV_WIDTH, lc] * xa_b
            for k in range(CONV_WIDTH - 1):
                lo = HALO - (CONV_WIDTH - 1) + k
                xc_b = xc_b + cw_ref[k:k + 1, lc] * xa_ref[blk, lo:lo + T, :]
            xa_ref[blk, 0:HALO, :] = xa_ref[blk, T:T + HALO, :]
            xc_blocks.append(xc_b)
        xc = jnp.concatenate(xc_blocks, axis=1)
        gates = _dot(xc.astype(bf16), pltpu.bitcast(gw_ref[g], bf16))

        q, k, v, gb = kind(KIND_Q), kind(KIND_K), kind(KIND_V), kind(KIND_GB)
        ga = kind(KIND_GA)
        cos = cos_ref[...]
        sin = sin_ref[...]
        q1, q2 = q[:, :half], q[:, half:]
        k1, k2 = k[:, :half], k[:, half:]
        qr = jnp.concatenate([q1 * cos - q2 * sin, q1 * sin + q2 * cos], axis=1).astype(bf16)
        kr = (jnp.concatenate([k1 * cos - k2 * sin, k1 * sin + k2 * cos], axis=1)
              * k_scale).astype(bf16)
        v_bf = v.astype(bf16)
        scores, kv = [], []
        for c in range(N_CHUNKS):
            rows = slice(c * CHUNK, (c + 1) * CHUNK)
            scores.append(lax.dot_general(qr[rows], kr[rows], (((1,), (1,)), ((), ())),
                                          preferred_element_type=jnp.float32))
            kv.append(lax.dot_general(kr[rows], (v[rows] * sdec_ref[:, cs]).astype(bf16),
                                      (((0,), (0,)), ((), ())),
                                      preferred_element_type=jnp.float32))

        if g + 1 < N_GROUPS:
            proj = project(g + 1)

        ti1 = 1.0 + jnp.tanh(gates[:, :GROUP] + 0.5 * gxb_ref[:, cs])
        tr1 = 1.0 + jnp.tanh(gates[:, GROUP:] + 0.5 * gab_ref[:, cs])
        nlam = -lam_ref[:, cs]
        softplus = jnp.maximum(nlam, 0.0) + jnp.log1p(jnp.exp(-jnp.abs(nlam)))
        nla = tr1 * ((0.5 * LRU_C) * softplus)
        a = jnp.exp(-nla)
        x4 = jnp.tanh(nla) * (a * (0.25 * a) + 0.25)
        mult = x4 * lax.rsqrt(jnp.maximum(x4, TINY))
        u = mult * (ti1 * xc)

        a3 = a.reshape(n_rg, SUBLANES, GROUP)
        u3 = u.reshape(n_rg, SUBLANES, GROUP)
        for s in (1, 2, 4):
            valid = sub >= s
            u_sh = jnp.where(valid, pltpu.roll(u3, s, 1), 0.0)
            a_sh = jnp.where(valid, pltpu.roll(a3, s, 1), 1.0)
            u3 = u3 + a3 * u_sh
            a3 = a3 * a_sh
        carry = hst_ref[:, cs]
        for j in range(n_rg):
            hj = u3[j] + a3[j] * carry
            lru_ref[j * SUBLANES:(j + 1) * SUBLANES, :] = hj
            carry = jnp.broadcast_to(hj[SUBLANES - 1:SUBLANES, :], (SUBLANES, GROUP))
        hst_ref[:, cs] = carry
        ya_ref[:, cs] = (ga * (1.0 + jnp.tanh(ga)) * lru_ref[...]).astype(bf16)

        silu_gb = gb * (1.0 + jnp.tanh(gb))
        gain = gn_ref[:, cs]
        r_state = r_ref[g]
        for c in range(N_CHUNKS):
            rows = slice(c * CHUNK, (c + 1) * CHUNK)
            cross = _dot(qr[rows], r_state.astype(bf16)) * cdec_ref[:, cs]
            r_state = chdec_ref[:, cs] * r_state + kv[c]
            inner = _dot((scores[c] * idec_ref[g]).astype(bf16), v_bf[rows])
            y = inner + cross
            mu = jnp.mean(y, axis=-1, keepdims=True)
            yc = y - mu
            var = jnp.mean(yc * yc, axis=-1, keepdims=True)
            gn = yc * lax.rsqrt(var + EPS) * gain
            yb_ref[rows, cs] = (silu_gb[rows] * gn).astype(bf16)
        r_ref[g] = r_state

    oa = _dot(ya_ref[...], wpa_s[...])
    ob = _dot(yb_ref[...], wpb_s[...])
    merged = (tma_ref[...] * oa + tmb_ref[...] * ob).astype(bf16)
    out_ref[0] = _dot(merged, wout_s[...])
    g_fin = nfin_ref[...]
    for r0 in range(0, T, ROW_CHUNK):
        xo = x_ref[0, r0:r0 + ROW_CHUNK, :] + out_ref[0, r0:r0 + ROW_CHUNK, :]
        if final_norm:
            ms = jnp.mean(xo * xo, axis=-1, keepdims=True)
            xo = xo * lax.rsqrt(ms + EPS) * g_fin
        out_ref[0, r0:r0 + ROW_CHUNK, :] = xo


def _block_diag_groups(w):
    per = GROUP // LRU_BW
    w4 = w.reshape(N_GROUPS, per, LRU_BW, LRU_BW)
    eye = jnp.eye(per, dtype=w.dtype)
    return jnp.einsum('gikn,ij->gikjn', w4, eye).reshape(N_GROUPS, GROUP, GROUP)


def _pack_rows(w):
    *lead, k, n = w.shape
    pairs = w.astype(jnp.bfloat16).reshape(*lead, k // 2, 2, n)
    return lax.bitcast_convert_type(jnp.swapaxes(pairs, -1, -2), jnp.uint32)


def _retention_tables(seq):
    f32 = jnp.float32
    half = RET_DK // 2
    freqs = ROPE_THETA ** (-jnp.arange(half, dtype=f32) / half)
    ang = jnp.arange(seq, dtype=f32)[:, None] * freqs[None, :]
    log_g = jnp.log1p(-(2.0 ** (-5.0 - jnp.arange(RET_HEADS, dtype=f32))))
    idx = jnp.arange(CHUNK, dtype=f32)
    diff = idx[:, None] - idx[None, :]
    inner = jnp.where(diff >= 0, jnp.exp(jnp.maximum(diff, 0.0)[None] * log_g[:, None, None]), 0.0)
    cross = jnp.exp((idx[:, None] + 1.0) * log_g[None, :])
    state = jnp.exp((CHUNK - 1.0 - idx[:, None]) * log_g[None, :])
    chunk = jnp.exp(CHUNK * log_g)[None, :]
    rep = lambda m: jnp.repeat(m, RET_DV, axis=1)
    return jnp.cos(ang), jnp.sin(ang), inner, rep(cross), rep(state), rep(chunk)


def _const_spec(shape):
    zeros = (0,) * len(shape)
    return pl.BlockSpec(shape, lambda b, t: zeros, pipeline_mode=pl.Buffered(1))


def _layer(x, nin, win, cw, cb, gxw, gxb, gaw, gab, lam, gn, wpa, wpb, wout, nfin, tables,
           final_norm):
    B, S, D = x.shape
    assert D == D_MODEL and S % TILE_T == 0
    assert win.shape == (D, N_KINDS * D) and wpa.shape == wpb.shape == wout.shape == (D, D)
    bf16 = jnp.bfloat16
    row = lambda v: v.reshape(1, -1)
    gw = _pack_rows(0.5 * jnp.concatenate([_block_diag_groups(gxw), _block_diag_groups(gaw)], axis=-1))
    cos, sin, idec, cdec, sdec, chdec = tables
    operands = (
        x, row(nin), win, cw, row(cb), gw, row(gxb), row(gab), row(lam), row(gn),
        wpa, wpb, wout, row(nfin), cos, sin, idec, cdec, sdec, chdec)
    hbm_operands = (2, 10, 11, 12)
    tile_spec = pl.BlockSpec((1, TILE_T, D), lambda b, t: (b, t, 0))
    pos_spec = pl.BlockSpec((TILE_T, RET_DK // 2), lambda b, t: (t, 0))
    in_specs = []
    for i, o in enumerate(operands):
        if i == 0:
            in_specs.append(tile_spec)
        elif i in hbm_operands:
            in_specs.append(pl.BlockSpec(memory_space=pl.ANY))
        elif i in (14, 15):
            in_specs.append(pos_spec)
        else:
            in_specs.append(_const_spec(o.shape))
    scratch = [
        pltpu.VMEM((D, N_KINDS * D), bf16),
        pltpu.VMEM((D, D), bf16),
        pltpu.VMEM((D, D), bf16),
        pltpu.VMEM((D, D), bf16),
        pltpu.VMEM((2, D, D), jnp.float32),
        pltpu.SemaphoreType.DMA((2,)),
        pltpu.VMEM((TILE_T, D), bf16),
        pltpu.VMEM((D // LANES, TILE_T + HALO, LANES), jnp.float32),
        pltpu.VMEM((TILE_T, GROUP), jnp.float32),
        pltpu.VMEM((TILE_T, D), bf16),
        pltpu.VMEM((TILE_T, D), bf16),
        pltpu.VMEM((TILE_T, D), jnp.float32),
        pltpu.VMEM((TILE_T, D), jnp.float32),
        pltpu.VMEM((RET_HEADS, RET_DK, RET_DV), jnp.float32),
        pltpu.VMEM((SUBLANES, D), jnp.float32),
    ]
    return pl.pallas_call(
        functools.partial(_block_kernel, final_norm=final_norm),
        grid=(B, S // TILE_T),
        in_specs=in_specs,
        out_specs=tile_spec,
        out_shape=jax.ShapeDtypeStruct(x.shape, x.dtype),
        scratch_shapes=scratch,
        compiler_params=pltpu.CompilerParams(
            dimension_semantics=("arbitrary", "arbitrary"),
            vmem_limit_bytes=VMEM_LIMIT_BYTES),
        name="hybrid_block",
    )(*operands)


def kernel(x, norm_in, w_in, conv_w, conv_b, gate_x_w, gate_x_b, gate_a_w, gate_a_b, lru_lambda, gn_gain, w_proj_a, w_proj_b, w_out, norm_final):
    depth = w_in.shape[0]
    tables = _retention_tables(x.shape[1])
    for l in range(depth):
        x = _layer(x, norm_in[l], w_in[l], conv_w[l], conv_b[l], gate_x_w[l], gate_x_b[l],
                   gate_a_w[l], gate_a_b[l], lru_lambda[l], gn_gain[l].reshape(-1),
                   w_proj_a[l], w_proj_b[l], w_out[l], norm_final, tables,
                   final_norm=(l == depth - 1))
    return x
```

```python
import functools

import jax
import jax.numpy as jnp
from jax import lax
from jax.experimental import pallas as pl
from jax.experimental.pallas import tpu as pltpu

D_MODEL = 1024
LRU_BLOCKS = 16
LRU_BW = D_MODEL // LRU_BLOCKS
CONV_WIDTH = 4
LRU_C = 8.0
RET_HEADS = 4
RET_DK = 256
RET_DV = 256
CHUNK = 256
ROPE_THETA = 10000.0
EPS = 1e-6

KIND_XA, KIND_GA, KIND_Q, KIND_K, KIND_V, KIND_GB, KIND_MA, KIND_MB = range(8)
N_KINDS = 8
HALVED_KINDS = (KIND_GA, KIND_GB, KIND_MA, KIND_MB)

LANES = 128
SUBLANES = 8
GROUP = 256
N_GROUPS = D_MODEL // GROUP
TILE_T = 256
N_CHUNKS = TILE_T // CHUNK
PROJ_AHEAD = 1
ROW_CHUNK = 32
PREP_ROWS = 64
HALO = SUBLANES
VMEM_LIMIT_BYTES = 56 * 1024 * 1024
TINY = 1e-37


def _dot(a, b):
    return jnp.dot(a, b, preferred_element_type=jnp.float32)


def _prepare_weights(win_hbm, wpa_hbm, wpb_hbm, wout_hbm, win_s, wpa_s, wpb_s, wout_s,
                     stage, sem):
    slabs = [(win_hbm.at[:, pl.ds(i * D_MODEL, D_MODEL)], i) for i in range(N_KINDS)]
    slabs += [(wpa_hbm, "pa"), (wpb_hbm, "pb"), (wout_hbm, "out")]

    def copy(n):
        return pltpu.make_async_copy(slabs[n][0], stage.at[n % 2], sem.at[n % 2])

    copy(0).start()
    for n, (_, tag) in enumerate(slabs):
        if n + 1 < len(slabs):
            copy(n + 1).start()
        copy(n).wait()
        slot = n % 2
        halve = tag in HALVED_KINDS or tag in ("pa", "pb")

        def convert(i, carry, slot=slot, tag=tag, halve=halve):
            rows = pl.ds(pl.multiple_of(i * PREP_ROWS, PREP_ROWS), PREP_ROWS)
            w = stage[slot, rows, :]
            if halve:
                w = 0.5 * w
            w = w.astype(jnp.bfloat16)
            if isinstance(tag, int):
                for g in range(N_GROUPS):
                    c0 = (g * N_KINDS + tag) * GROUP
                    win_s[rows, c0:c0 + GROUP] = w[:, g * GROUP:(g + 1) * GROUP]
            else:
                {"pa": wpa_s, "pb": wpb_s, "out": wout_s}[tag][rows, :] = w
            return carry

        lax.fori_loop(0, D_MODEL // PREP_ROWS, convert, 0)


def _block_kernel(x_ref, xn_ref, nin_ref, win_hbm, cw_ref, cb_ref, gw_ref, gxb_ref, gab_ref,
                  lam_ref, gn_ref, wpa_hbm, wpb_hbm, wout_hbm, nfin_ref, cos_ref, sin_ref,
                  idec_ref, cdec_ref, sdec_ref, chdec_ref, out_ref,
                  win_s, wpa_s, wpb_s, wout_s, stage, sem,
                  h_ref, xo_ref, xa_ref, lru_ref, ya_ref, yb_ref, tma_ref, tmb_ref, r_ref,
                  hst_ref, *, final_norm, tiles_per_seq):
    s = pl.program_id(0)
    n_tiles = pl.num_programs(0) - 1

    def project(g):
        w0 = g * N_KINDS * GROUP
        return _dot(h_ref[...], win_s[:, w0:w0 + N_KINDS * GROUP])

    def input_norm(src_ref):
        g_in = nin_ref[...]
        for r0 in range(0, TILE_T, ROW_CHUNK):
            xr = src_ref[0, r0:r0 + ROW_CHUNK, :]
            ms = jnp.mean(xr * xr, axis=-1, keepdims=True)
            h_ref[r0:r0 + ROW_CHUNK, :] = (
                xr * lax.rsqrt(ms + EPS) * g_in).astype(jnp.bfloat16)

    def output_stage():
        g_fin = nfin_ref[...]
        for r0 in range(0, TILE_T, ROW_CHUNK):
            xo = xo_ref[r0:r0 + ROW_CHUNK, :]
            if final_norm:
                ms = jnp.mean(xo * xo, axis=-1, keepdims=True)
                xo = xo * lax.rsqrt(ms + EPS) * g_fin
            out_ref[0, r0:r0 + ROW_CHUNK, :] = xo

    @pl.when(s == 0)
    def _():
        _prepare_weights(win_hbm, wpa_hbm, wpb_hbm, wout_hbm, win_s, wpa_s, wpb_s, wout_s,
                         stage, sem)
        input_norm(x_ref)
        xo_ref[...] = jnp.zeros_like(xo_ref)

    @pl.when(s < n_tiles)
    def _():
        _main_stage(s % tiles_per_seq, x_ref, xn_ref, cw_ref, cb_ref, gw_ref, gxb_ref, gab_ref,
                    lam_ref, gn_ref, cos_ref, sin_ref, idec_ref, cdec_ref, sdec_ref, chdec_ref,
                    wpa_s, wpb_s, wout_s, xo_ref, xa_ref, lru_ref, ya_ref,
                    yb_ref, tma_ref, tmb_ref, r_ref, hst_ref, input_norm, output_stage, project)

    @pl.when(s == n_tiles)
    def _():
        output_stage()


def _main_stage(t, x_ref, xn_ref, cw_ref, cb_ref, gw_ref, gxb_ref, gab_ref, lam_ref, gn_ref,
                cos_ref, sin_ref, idec_ref, cdec_ref, sdec_ref, chdec_ref,
                wpa_s, wpb_s, wout_s, xo_ref, xa_ref, lru_ref, ya_ref, yb_ref,
                tma_ref, tmb_ref, r_ref, hst_ref, input_norm, output_stage, project):
    T = TILE_T
    bf16 = jnp.bfloat16

    @pl.when(t == 0)
    def _():
        r_ref[...] = jnp.zeros_like(r_ref)
        hst_ref[...] = jnp.zeros_like(hst_ref)
        xa_ref[:, 0:HALO, :] = jnp.zeros((D_MODEL // LANES, HALO, LANES), jnp.float32)

    output_stage()

    n_rg = T // SUBLANES
    sub = lax.broadcasted_iota(jnp.int32, (n_rg, SUBLANES, GROUP), 1)
    half = RET_DK // 2
    k_scale = RET_DK ** -0.5

    projs = {g: project(g) for g in range(PROJ_AHEAD)}
    for g in range(N_GROUPS):
        cs = slice(g * GROUP, (g + 1) * GROUP)
        kind = lambda i, p=projs.pop(g): p[:, i * GROUP:(i + 1) * GROUP]
        tma_ref[:, cs] = 1.0 + jnp.tanh(kind(KIND_MA))
        tmb_ref[:, cs] = 1.0 + jnp.tanh(kind(KIND_MB))

        xa = kind(KIND_XA)
        xc_blocks = []
        for b in range(GROUP // LANES):
            blk = g * (GROUP // LANES) + b
            lc = slice(blk * LANES, (blk + 1) * LANES)
            xa_b = xa[:, b * LANES:(b + 1) * LANES]
            xa_ref[blk, HALO:HALO + T, :] = xa_b
            xc_b = cb_ref[:, lc] + cw_ref[CONV_WIDTH - 1:CONV_WIDTH, lc] * xa_b
            for k in range(CONV_WIDTH - 1):
                lo = HALO - (CONV_WIDTH - 1) + k
                xc_b = xc_b + cw_ref[k:k + 1, lc] * xa_ref[blk, lo:lo + T, :]
            xa_ref[blk, 0:HALO, :] = xa_ref[blk, T:T + HALO, :]
            xc_blocks.append(xc_b)
        xc = jnp.concatenate(xc_blocks, axis=1)
        gates = _dot(xc.astype(bf16), pltpu.bitcast(gw_ref[g], bf16))

        q, k, v, gb = kind(KIND_Q), kind(KIND_K), kind(KIND_V), kind(KIND_GB)
        ga = kind(KIND_GA)
        cos = cos_ref[...]
        sin = sin_ref[...]
        q1, q2 = q[:, :half], q[:, half:]
        k1, k2 = k[:, :half], k[:, half:]
        qr = jnp.concatenate([q1 * cos - q2 * sin, q1 * sin + q2 * cos], axis=1).astype(bf16)
        kr = (jnp.concatenate([k1 * cos - k2 * sin, k1 * sin + k2 * cos], axis=1)
              * k_scale).astype(bf16)
        v_bf = v.astype(bf16)
        scores, kv = [], []
        for c in range(N_CHUNKS):
            rows = slice(c * CHUNK, (c + 1) * CHUNK)
            scores.append(lax.dot_general(qr[rows], kr[rows], (((1,), (1,)), ((), ())),
                                          preferred_element_type=jnp.float32))
            kv.append(lax.dot_general(kr[rows], (v[rows] * sdec_ref[:, cs]).astype(bf16),
                                      (((0,), (0,)), ((), ())),
                                      preferred_element_type=jnp.float32))

        r_state = r_ref[g]
        cross = []
        for c in range(N_CHUNKS):
            rows = slice(c * CHUNK, (c + 1) * CHUNK)
            cross.append(_dot(qr[rows], r_state.astype(bf16)) * cdec_ref[:, cs])
            r_state = chdec_ref[:, cs] * r_state + kv[c]
        r_ref[g] = r_state

        if g + PROJ_AHEAD < N_GROUPS:
            projs[g + PROJ_AHEAD] = project(g + PROJ_AHEAD)

        ti1 = 1.0 + jnp.tanh(gates[:, :GROUP] + 0.5 * gxb_ref[:, cs])
        tr1 = 1.0 + jnp.tanh(gates[:, GROUP:] + 0.5 * gab_ref[:, cs])
        nlam = -lam_ref[:, cs]
        softplus = jnp.maximum(nlam, 0.0) + jnp.log1p(jnp.exp(-jnp.abs(nlam)))
        nla = tr1 * ((0.5 * LRU_C) * softplus)
        a = jnp.exp(-nla)
        x4 = jnp.tanh(nla) * (a * (0.25 * a) + 0.25)
        mult = x4 * lax.rsqrt(jnp.maximum(x4, TINY))
        u = mult * (ti1 * xc)

        a3 = a.reshape(n_rg, SUBLANES, GROUP)
        u3 = u.reshape(n_rg, SUBLANES, GROUP)
        for s in (1, 2, 4):
            valid = sub >= s
            u_sh = jnp.where(valid, pltpu.roll(u3, s, 1), 0.0)
            a_sh = jnp.where(valid, pltpu.roll(a3, s, 1), 1.0)
            u3 = u3 + a3 * u_sh
            a3 = a3 * a_sh
        carry = hst_ref[:, cs]
        for j in range(n_rg):
            hj = u3[j] + a3[j] * carry
            lru_ref[j * SUBLANES:(j + 1) * SUBLANES, :] = hj
            carry = jnp.broadcast_to(hj[SUBLANES - 1:SUBLANES, :], (SUBLANES, GROUP))
        hst_ref[:, cs] = carry
        ya_ref[:, cs] = (ga * (1.0 + jnp.tanh(ga)) * lru_ref[...]).astype(bf16)

        silu_gb = gb * (1.0 + jnp.tanh(gb))
        gain = gn_ref[:, cs]
        for c in range(N_CHUNKS):
            rows = slice(c * CHUNK, (c + 1) * CHUNK)
            inner = _dot((scores[c] * idec_ref[g]).astype(bf16), v_bf[rows])
            y = inner + cross[c]
            mu = jnp.mean(y, axis=-1, keepdims=True)
            yc = y - mu
            var = jnp.mean(yc * yc, axis=-1, keepdims=True)
            gn = yc * lax.rsqrt(var + EPS) * gain
            yb_ref[rows, cs] = (silu_gb[rows] * gn).astype(bf16)

    oa = _dot(ya_ref[...], wpa_s[...])
    ob = _dot(yb_ref[...], wpb_s[...])
    merged = (tma_ref[...] * oa + tmb_ref[...] * ob).astype(bf16)
    xo_ref[...] = x_ref[0] + _dot(merged, wout_s[...])

    input_norm(xn_ref)


def _block_diag_groups(w):
    per = GROUP // LRU_BW
    w4 = w.reshape(N_GROUPS, per, LRU_BW, LRU_BW)
    eye = jnp.eye(per, dtype=w.dtype)
    return jnp.einsum('gikn,ij->gikjn', w4, eye).reshape(N_GROUPS, GROUP, GROUP)


def _pack_rows(w):
    *lead, k, n = w.shape
    pairs = w.astype(jnp.bfloat16).reshape(*lead, k // 2, 2, n)
    return lax.bitcast_convert_type(jnp.swapaxes(pairs, -1, -2), jnp.uint32)


def _retention_tables(seq):
    f32 = jnp.float32
    half = RET_DK // 2
    freqs = ROPE_THETA ** (-jnp.arange(half, dtype=f32) / half)
    ang = jnp.arange(seq, dtype=f32)[:, None] * freqs[None, :]
    log_g = jnp.log1p(-(2.0 ** (-5.0 - jnp.arange(RET_HEADS, dtype=f32))))
    idx = jnp.arange(CHUNK, dtype=f32)
    diff = idx[:, None] - idx[None, :]
    inner = jnp.where(diff >= 0, jnp.exp(jnp.maximum(diff, 0.0)[None] * log_g[:, None, None]), 0.0)
    cross = jnp.exp((idx[:, None] + 1.0) * log_g[None, :])
    state = jnp.exp((CHUNK - 1.0 - idx[:, None]) * log_g[None, :])
    chunk = jnp.exp(CHUNK * log_g)[None, :]
    rep = lambda m: jnp.repeat(m, RET_DV, axis=1)
    return jnp.cos(ang), jnp.sin(ang), inner, rep(cross), rep(state), rep(chunk)


def _const_spec(shape):
    zeros = (0,) * len(shape)
    return pl.BlockSpec(shape, lambda s: zeros, pipeline_mode=pl.Buffered(1))


def _layer(x, nin, win, cw, cb, gxw, gxb, gaw, gab, lam, gn, wpa, wpb, wout, nfin, tables,
           final_norm):
    B, S, D = x.shape
    assert D == D_MODEL and S % TILE_T == 0
    assert win.shape == (D, N_KINDS * D) and wpa.shape == wpb.shape == wout.shape == (D, D)
    bf16 = jnp.bfloat16
    row = lambda v: v.reshape(1, -1)
    gw = _pack_rows(0.5 * jnp.concatenate([_block_diag_groups(gxw), _block_diag_groups(gaw)], axis=-1))
    cos, sin, idec, cdec, sdec, chdec = tables
    operands = (
        x, x, row(nin), win, cw, row(cb), gw, row(gxb), row(gab), row(lam), row(gn),
        wpa, wpb, wout, row(nfin), cos, sin, idec, cdec, sdec, chdec)
    hbm_operands = (3, 11, 12, 13)
    nt = S // TILE_T
    n_tiles = B * nt
    tile_of = lambda i: (i // nt, i % nt, 0)
    cur_spec = pl.BlockSpec((1, TILE_T, D), lambda s: tile_of(jnp.minimum(s, n_tiles - 1)))
    nxt_spec = pl.BlockSpec((1, TILE_T, D), lambda s: tile_of(jnp.minimum(s + 1, n_tiles - 1)))
    out_spec = pl.BlockSpec((1, TILE_T, D), lambda s: tile_of(jnp.maximum(s - 1, 0)))
    pos_spec = pl.BlockSpec((TILE_T, RET_DK // 2),
                            lambda s: (jnp.minimum(s, n_tiles - 1) % nt, 0))
    in_specs = []
    for i, o in enumerate(operands):
        if i == 0:
            in_specs.append(cur_spec)
        elif i == 1:
            in_specs.append(nxt_spec)
        elif i in hbm_operands:
            in_specs.append(pl.BlockSpec(memory_space=pl.ANY))
        elif i in (15, 16):
            in_specs.append(pos_spec)
        else:
            in_specs.append(_const_spec(o.shape))
    scratch = [
        pltpu.VMEM((D, N_KINDS * D), bf16),
        pltpu.VMEM((D, D), bf16),
        pltpu.VMEM((D, D), bf16),
        pltpu.VMEM((D, D), bf16),
        pltpu.VMEM((2, D, D), jnp.float32),
        pltpu.SemaphoreType.DMA((2,)),
        pltpu.VMEM((TILE_T, D), bf16),
        pltpu.VMEM((TILE_T, D), jnp.float32),
        pltpu.VMEM((D // LANES, TILE_T + HALO, LANES), jnp.float32),
        pltpu.VMEM((TILE_T, GROUP), jnp.float32),
        pltpu.VMEM((TILE_T, D), bf16),
        pltpu.VMEM((TILE_T, D), bf16),
        pltpu.VMEM((TILE_T, D), jnp.float32),
        pltpu.VMEM((TILE_T, D), jnp.float32),
        pltpu.VMEM((RET_HEADS, RET_DK, RET_DV), jnp.float32),
        pltpu.VMEM((SUBLANES, D), jnp.float32),
    ]
    return pl.pallas_call(
        functools.partial(_block_kernel, final_norm=final_norm, tiles_per_seq=nt),
        grid=(n_tiles + 1,),
        in_specs=in_specs,
        out_specs=out_spec,
        out_shape=jax.ShapeDtypeStruct(x.shape, x.dtype),
        scratch_shapes=scratch,
        compiler_params=pltpu.CompilerParams(
            dimension_semantics=("arbitrary",),
            vmem_limit_bytes=VMEM_LIMIT_BYTES),
        name="hybrid_block",
    )(*operands)


def kernel(x, norm_in, w_in, conv_w, conv_b, gate_x_w, gate_x_b, gate_a_w, gate_a_b, lru_lambda, gn_gain, w_proj_a, w_proj_b, w_out, norm_final):
    depth = w_in.shape[0]
    tables = _retention_tables(x.shape[1])
    for l in range(depth):
        x = _layer(x, norm_in[l], w_in[l], conv_w[l], conv_b[l], gate_x_w[l], gate_x_b[l],
                   gate_a_w[l], gate_a_b[l], lru_lambda[l], gn_gain[l].reshape(-1),
                   w_proj_a[l], w_proj_b[l], w_out[l], norm_final, tables,
                   final_norm=(l == depth - 1))
    return x
```

```python
import functools

import jax
import jax.numpy as jnp
from jax import lax
from jax.experimental import pallas as pl
from jax.experimental.pallas import tpu as pltpu

D_MODEL = 1024
LRU_BLOCKS = 16
LRU_BW = D_MODEL // LRU_BLOCKS
CONV_WIDTH = 4
LRU_C = 8.0
RET_HEADS = 4
RET_DK = 256
RET_DV = 256
CHUNK = 256
ROPE_THETA = 10000.0
EPS = 1e-6

KIND_XA, KIND_GA, KIND_Q, KIND_K, KIND_V, KIND_GB, KIND_MA, KIND_MB = range(8)
N_KINDS = 8
N_GROUP_KINDS = 6
HALVED_KINDS = (KIND_GA, KIND_GB, KIND_MA, KIND_MB)

LANES = 128
SUBLANES = 8
GROUP = 256
N_GROUPS = D_MODEL // GROUP
GROUP_COLS = N_GROUP_KINDS * GROUP
MERGE_COL0 = N_GROUPS * GROUP_COLS
TILE_T = 256
N_CHUNKS = TILE_T // CHUNK
ROW_CHUNK = 32
STAGE_COLS = 512
PREP_ROWS = 64
HALO = SUBLANES
VMEM_LIMIT_BYTES = 56 * 1024 * 1024
TINY = 1e-37


def _dot(a, b):
    return jnp.dot(a, b, preferred_element_type=jnp.float32)


def _prepare_weights(win_hbm, wpa_hbm, wpb_hbm, wout_hbm, win_s, wpa_s, wpb_s, wout_s,
                     stage, sem):
    slabs = []
    for kind in range(N_KINDS):
        for c0 in range(0, D_MODEL, STAGE_COLS):
            src = win_hbm.at[:, pl.ds(kind * D_MODEL + c0, STAGE_COLS)]
            if kind < N_GROUP_KINDS:
                pieces = [(c - c0, (c // GROUP) * GROUP_COLS + kind * GROUP, GROUP)
                          for c in range(c0, c0 + STAGE_COLS, GROUP)]
            else:
                pieces = [(0, kind * D_MODEL + c0, STAGE_COLS)]
            slabs.append((src, win_s, pieces, kind in HALVED_KINDS))
    for src_hbm, dst, halve in ((wpa_hbm, wpa_s, True), (wpb_hbm, wpb_s, True),
                                (wout_hbm, wout_s, False)):
        for c0 in range(0, D_MODEL, STAGE_COLS):
            slabs.append((src_hbm.at[:, pl.ds(c0, STAGE_COLS)], dst,
                          [(0, c0, STAGE_COLS)], halve))

    def copy(n):
        return pltpu.make_async_copy(slabs[n][0], stage.at[n % 2], sem.at[n % 2])

    copy(0).start()
    for n, (_, dst, pieces, halve) in enumerate(slabs):
        if n + 1 < len(slabs):
            copy(n + 1).start()
        copy(n).wait()

        def convert(i, carry, slot=n % 2, dst=dst, pieces=pieces, halve=halve):
            rows = pl.ds(pl.multiple_of(i * PREP_ROWS, PREP_ROWS), PREP_ROWS)
            w = stage[slot, rows, :]
            if halve:
                w = 0.5 * w
            w = w.astype(jnp.bfloat16)
            for s0, d0, width in pieces:
                dst[rows, d0:d0 + width] = w[:, s0:s0 + width]
            return carry

        lax.fori_loop(0, D_MODEL // PREP_ROWS, convert, 0)


def _block_kernel(x_ref, nin_ref, win_hbm, cw_ref, cb_ref, gw_ref, gxb_ref, gab_ref, lam_ref,
                  gn_ref, wpa_hbm, wpb_hbm, wout_hbm, nfin_ref, cos_ref, sin_ref, idec_ref,
                  cdec_ref, sdec_ref, chdec_ref, out_ref,
                  win_s, wpa_s, wpb_s, wout_s, stage, sem,
                  h_ref, xa_ref, lru_ref, ya_ref, yb_ref, r_ref, hst_ref,
                  *, final_norm):
    t = pl.program_id(1)
    T = TILE_T
    bf16 = jnp.bfloat16

    @pl.when((pl.program_id(0) == 0) & (t == 0))
    def _():
        _prepare_weights(win_hbm, wpa_hbm, wpb_hbm, wout_hbm, win_s, wpa_s, wpb_s, wout_s,
                         stage, sem)

    @pl.when(t == 0)
    def _():
        r_ref[...] = jnp.zeros_like(r_ref)
        hst_ref[...] = jnp.zeros_like(hst_ref)
        xa_ref[:, 0:HALO, :] = jnp.zeros((D_MODEL // LANES, HALO, LANES), jnp.float32)

    g_in = nin_ref[...]
    for r0 in range(0, T, ROW_CHUNK):
        xr = x_ref[0, r0:r0 + ROW_CHUNK, :]
        ms = jnp.mean(xr * xr, axis=-1, keepdims=True)
        h_ref[r0:r0 + ROW_CHUNK, :] = (xr * lax.rsqrt(ms + EPS) * g_in).astype(bf16)

    n_rg = T // SUBLANES
    sub = lax.broadcasted_iota(jnp.int32, (n_rg, SUBLANES, GROUP), 1)
    half = RET_DK // 2
    k_scale = RET_DK ** -0.5

    def project(g):
        return _dot(h_ref[...], win_s[:, g * GROUP_COLS:(g + 1) * GROUP_COLS])

    proj = project(0)
    for g in range(N_GROUPS):
        cs = slice(g * GROUP, (g + 1) * GROUP)
        kind = lambda i, p=proj: p[:, i * GROUP:(i + 1) * GROUP]

        xa = kind(KIND_XA)
        xc_blocks = []
        for b in range(GROUP // LANES):
            blk = g * (GROUP // LANES) + b
            lc = slice(blk * LANES, (blk + 1) * LANES)
            xa_b = xa[:, b * LANES:(b + 1) * LANES]
            xa_ref[blk, HALO:HALO + T, :] = xa_b
            xc_b = cb_ref[:, lc] + cw_ref[CONV_WIDTH - 1:CONV_WIDTH, lc] * xa_b
            for k in range(CONV_WIDTH - 1):
                lo = HALO - (CONV_WIDTH - 1) + k
                xc_b = xc_b + cw_ref[k:k + 1, lc] * xa_ref[blk, lo:lo + T, :]
            xa_ref[blk, 0:HALO, :] = xa_ref[blk, T:T + HALO, :]
            xc_blocks.append(xc_b)
        xc = jnp.concatenate(xc_blocks, axis=1)
        gates = _dot(xc.astype(bf16), pltpu.bitcast(gw_ref[g], bf16))

        q, k, v, gb = kind(KIND_Q), kind(KIND_K), kind(KIND_V), kind(KIND_GB)
        ga = kind(KIND_GA)
        cos = cos_ref[...]
        sin = sin_ref[...]
        q1, q2 = q[:, :half], q[:, half:]
        k1, k2 = k[:, :half], k[:, half:]
        qr = jnp.concatenate([q1 * cos - q2 * sin, q1 * sin + q2 * cos], axis=1).astype(bf16)
        kr = (jnp.concatenate([k1 * cos - k2 * sin, k1 * sin + k2 * cos], axis=1)
              * k_scale).astype(bf16)
        v_bf = v.astype(bf16)
        scores, kv = [], []
        for c in range(N_CHUNKS):
            rows = slice(c * CHUNK, (c + 1) * CHUNK)
            scores.append(lax.dot_general(qr[rows], kr[rows], (((1,), (1,)), ((), ())),
                                          preferred_element_type=jnp.float32))
            kv.append(lax.dot_general(kr[rows], (v[rows] * sdec_ref[:, cs]).astype(bf16),
                                      (((0,), (0,)), ((), ())),
                                      preferred_element_type=jnp.float32))

        if g + 1 < N_GROUPS:
            proj = project(g + 1)

        ti1 = 1.0 + jnp.tanh(gates[:, :GROUP] + 0.5 * gxb_ref[:, cs])
        tr1 = 1.0 + jnp.tanh(gates[:, GROUP:] + 0.5 * gab_ref[:, cs])
        nlam = -lam_ref[:, cs]
        softplus = jnp.maximum(nlam, 0.0) + jnp.log1p(jnp.exp(-jnp.abs(nlam)))
        nla = tr1 * ((0.5 * LRU_C) * softplus)
        a = jnp.exp(-nla)
        x4 = jnp.tanh(nla) * (a * (0.25 * a) + 0.25)
        mult = x4 * lax.rsqrt(jnp.maximum(x4, TINY))
        u = mult * (ti1 * xc)

        a3 = a.reshape(n_rg, SUBLANES, GROUP)
        u3 = u.reshape(n_rg, SUBLANES, GROUP)
        for s in (1, 2, 4):
            valid = sub >= s
            u_sh = jnp.where(valid, pltpu.roll(u3, s, 1), 0.0)
            a_sh = jnp.where(valid, pltpu.roll(a3, s, 1), 1.0)
            u3 = u3 + a3 * u_sh
            a3 = a3 * a_sh
        carry = hst_ref[:, cs]
        for j in range(n_rg):
            hj = u3[j] + a3[j] * carry
            lru_ref[j * SUBLANES:(j + 1) * SUBLANES, :] = hj
            carry = jnp.broadcast_to(hj[SUBLANES - 1:SUBLANES, :], (SUBLANES, GROUP))
        hst_ref[:, cs] = carry
        ya_ref[:, cs] = (ga * (1.0 + jnp.tanh(ga)) * lru_ref[...]).astype(bf16)

        silu_gb = gb * (1.0 + jnp.tanh(gb))
        gain = gn_ref[:, cs]
        r_state = r_ref[g]
        for c in range(N_CHUNKS):
            rows = slice(c * CHUNK, (c + 1) * CHUNK)
            cross = _dot(qr[rows], r_state.astype(bf16)) * cdec_ref[:, cs]
            r_state = chdec_ref[:, cs] * r_state + kv[c]
            inner = _dot((scores[c] * idec_ref[g]).astype(bf16), v_bf[rows])
            y = inner + cross
            mu = jnp.mean(y, axis=-1, keepdims=True)
            yc = y - mu
            var = jnp.mean(yc * yc, axis=-1, keepdims=True)
            gn = yc * lax.rsqrt(var + EPS) * gain
            yb_ref[rows, cs] = (silu_gb[rows] * gn).astype(bf16)
        r_ref[g] = r_state

    oa = _dot(ya_ref[...], wpa_s[...])
    ob = _dot(yb_ref[...], wpb_s[...])
    mg = _dot(h_ref[...], win_s[:, MERGE_COL0:MERGE_COL0 + 2 * D_MODEL])
    merged = ((1.0 + jnp.tanh(mg[:, :D_MODEL])) * oa
              + (1.0 + jnp.tanh(mg[:, D_MODEL:])) * ob).astype(bf16)
    out_ref[0] = _dot(merged, wout_s[...])
    g_fin = nfin_ref[...]
    for r0 in range(0, T, ROW_CHUNK):
        xo = x_ref[0, r0:r0 + ROW_CHUNK, :] + out_ref[0, r0:r0 + ROW_CHUNK, :]
        if final_norm:
            ms = jnp.mean(xo * xo, axis=-1, keepdims=True)
            xo = xo * lax.rsqrt(ms + EPS) * g_fin
        out_ref[0, r0:r0 + ROW_CHUNK, :] = xo


def _block_diag_groups(w):
    per = GROUP // LRU_BW
    w4 = w.reshape(N_GROUPS, per, LRU_BW, LRU_BW)
    eye = jnp.eye(per, dtype=w.dtype)
    return jnp.einsum('gikn,ij->gikjn', w4, eye).reshape(N_GROUPS, GROUP, GROUP)


def _pack_rows(w):
    *lead, k, n = w.shape
    pairs = w.astype(jnp.bfloat16).reshape(*lead, k // 2, 2, n)
    return lax.bitcast_convert_type(jnp.swapaxes(pairs, -1, -2), jnp.uint32)


def _retention_tables(seq):
    f32 = jnp.float32
    half = RET_DK // 2
    freqs = ROPE_THETA ** (-jnp.arange(half, dtype=f32) / half)
    ang = jnp.arange(seq, dtype=f32)[:, None] * freqs[None, :]
    log_g = jnp.log1p(-(2.0 ** (-5.0 - jnp.arange(RET_HEADS, dtype=f32))))
    idx = jnp.arange(CHUNK, dtype=f32)
    diff = idx[:, None] - idx[None, :]
    inner = jnp.where(diff >= 0, jnp.exp(jnp.maximum(diff, 0.0)[None] * log_g[:, None, None]), 0.0)
    cross = jnp.exp((idx[:, None] + 1.0) * log_g[None, :])
    state = jnp.exp((CHUNK - 1.0 - idx[:, None]) * log_g[None, :])
    chunk = jnp.exp(CHUNK * log_g)[None, :]
    rep = lambda m: jnp.repeat(m, RET_DV, axis=1)
    return jnp.cos(ang), jnp.sin(ang), inner, rep(cross), rep(state), rep(chunk)


def _const_spec(shape):
    zeros = (0,) * len(shape)
    return pl.BlockSpec(shape, lambda b, t: zeros, pipeline_mode=pl.Buffered(1))


def _layer(x, nin, win, cw, cb, gxw, gxb, gaw, gab, lam, gn, wpa, wpb, wout, nfin, tables,
           final_norm):
    B, S, D = x.shape
    assert D == D_MODEL and S % TILE_T == 0
    assert win.shape == (D, N_KINDS * D) and wpa.shape == wpb.shape == wout.shape == (D, D)
    bf16 = jnp.bfloat16
    row = lambda v: v.reshape(1, -1)
    gw = _pack_rows(0.5 * jnp.concatenate([_block_diag_groups(gxw), _block_diag_groups(gaw)], axis=-1))
    cos, sin, idec, cdec, sdec, chdec = tables
    operands = (
        x, row(nin), win, cw, row(cb), gw, row(gxb), row(gab), row(lam), row(gn),
        wpa, wpb, wout, row(nfin), cos, sin, idec, cdec, sdec, chdec)
    hbm_operands = (2, 10, 11, 12)
    tile_spec = pl.BlockSpec((1, TILE_T, D), lambda b, t: (b, t, 0))
    pos_spec = pl.BlockSpec((TILE_T, RET_DK // 2), lambda b, t: (t, 0))
    in_specs = []
    for i, o in enumerate(operands):
        if i == 0:
            in_specs.append(tile_spec)
        elif i in hbm_operands:
            in_specs.append(pl.BlockSpec(memory_space=pl.ANY))
        elif i in (14, 15):
            in_specs.append(pos_spec)
        else:
            in_specs.append(_const_spec(o.shape))
    scratch = [
        pltpu.VMEM((D, N_KINDS * D), bf16),
        pltpu.VMEM((D, D), bf16),
        pltpu.VMEM((D, D), bf16),
        pltpu.VMEM((D, D), bf16),
        pltpu.VMEM((2, D, STAGE_COLS), jnp.float32),
        pltpu.SemaphoreType.DMA((2,)),
        pltpu.VMEM((TILE_T, D), bf16),
        pltpu.VMEM((D // LANES, TILE_T + HALO, LANES), jnp.float32),
        pltpu.VMEM((TILE_T, GROUP), jnp.float32),
        pltpu.VMEM((TILE_T, D), bf16),
        pltpu.VMEM((TILE_T, D), bf16),
        pltpu.VMEM((RET_HEADS, RET_DK, RET_DV), jnp.float32),
        pltpu.VMEM((SUBLANES, D), jnp.float32),
    ]
    return pl.pallas_call(
        functools.partial(_block_kernel, final_norm=final_norm),
        grid=(B, S // TILE_T),
        in_specs=in_specs,
        out_specs=tile_spec,
        out_shape=jax.ShapeDtypeStruct(x.shape, x.dtype),
        scratch_shapes=scratch,
        compiler_params=pltpu.CompilerParams(
            dimension_semantics=("arbitrary", "arbitrary"),
            vmem_limit_bytes=VMEM_LIMIT_BYTES),
        name="hybrid_block",
    )(*operands)


def kernel(x, norm_in, w_in, conv_w, conv_b, gate_x_w, gate_x_b, gate_a_w, gate_a_b, lru_lambda, gn_gain, w_proj_a, w_proj_b, w_out, norm_final):
    depth = w_in.shape[0]
    tables = _retention_tables(x.shape[1])
    for l in range(depth):
        x = _layer(x, norm_in[l], w_in[l], conv_w[l], conv_b[l], gate_x_w[l], gate_x_b[l],
                   gate_a_w[l], gate_a_b[l], lru_lambda[l], gn_gain[l].reshape(-1),
                   w_proj_a[l], w_proj_b[l], w_out[l], norm_final, tables,
                   final_norm=(l == depth - 1))
    return x
```

```python
import functools

import jax
import jax.numpy as jnp
from jax import lax
from jax.experimental import pallas as pl
from jax.experimental.pallas import tpu as pltpu

D_MODEL = 1024
LRU_BLOCKS = 16
LRU_BW = D_MODEL // LRU_BLOCKS
CONV_WIDTH = 4
LRU_C = 8.0
RET_HEADS = 4
RET_DK = 256
RET_DV = 256
CHUNK = 256
ROPE_THETA = 10000.0
EPS = 1e-6

KIND_XA, KIND_GA, KIND_Q, KIND_K, KIND_V, KIND_GB, KIND_MA, KIND_MB = range(8)
N_KINDS = 8
N_GROUP_KINDS = 6
HALVED_KINDS = (KIND_GA, KIND_GB, KIND_MA, KIND_MB)

LANES = 128
SUBLANES = 8
GROUP = 256
N_GROUPS = D_MODEL // GROUP
GROUP_COLS = N_GROUP_KINDS * GROUP
MERGE_COL0 = N_GROUPS * GROUP_COLS
TILE_T = 512
N_CHUNKS = TILE_T // CHUNK
ROW_CHUNK = 32
STAGE_COLS = 512
PREP_ROWS = 64
HALO = SUBLANES
VMEM_LIMIT_BYTES = 56 * 1024 * 1024
TINY = 1e-37


def _dot(a, b):
    return jnp.dot(a, b, preferred_element_type=jnp.float32)


def _prepare_weights(win_hbm, wpa_hbm, wpb_hbm, wout_hbm, win_s, wpa_s, wpb_s, wout_s,
                     stage, sem):
    slabs = []
    for kind in range(N_KINDS):
        for c0 in range(0, D_MODEL, STAGE_COLS):
            src = win_hbm.at[:, pl.ds(kind * D_MODEL + c0, STAGE_COLS)]
            if kind < N_GROUP_KINDS:
                pieces = [(c - c0, (c // GROUP) * GROUP_COLS + kind * GROUP, GROUP)
                          for c in range(c0, c0 + STAGE_COLS, GROUP)]
            else:
                pieces = [(0, kind * D_MODEL + c0, STAGE_COLS)]
            slabs.append((src, win_s, pieces, kind in HALVED_KINDS))
    for src_hbm, dst, halve in ((wpa_hbm, wpa_s, True), (wpb_hbm, wpb_s, True),
                                (wout_hbm, wout_s, False)):
        for c0 in range(0, D_MODEL, STAGE_COLS):
            slabs.append((src_hbm.at[:, pl.ds(c0, STAGE_COLS)], dst,
                          [(0, c0, STAGE_COLS)], halve))

    def copy(n):
        return pltpu.make_async_copy(slabs[n][0], stage.at[n % 2], sem.at[n % 2])

    copy(0).start()
    for n, (_, dst, pieces, halve) in enumerate(slabs):
        if n + 1 < len(slabs):
            copy(n + 1).start()
        copy(n).wait()

        def convert(i, carry, slot=n % 2, dst=dst, pieces=pieces, halve=halve):
            rows = pl.ds(pl.multiple_of(i * PREP_ROWS, PREP_ROWS), PREP_ROWS)
            w = stage[slot, rows, :]
            if halve:
                w = 0.5 * w
            w = w.astype(jnp.bfloat16)
            for s0, d0, width in pieces:
                dst[rows, d0:d0 + width] = w[:, s0:s0 + width]
            return carry

        lax.fori_loop(0, D_MODEL // PREP_ROWS, convert, 0)


def _block_kernel(x_ref, nin_ref, win_hbm, cw_ref, cb_ref, gw_ref, gxb_ref, gab_ref, lam_ref,
                  gn_ref, wpa_hbm, wpb_hbm, wout_hbm, nfin_ref, cos_ref, sin_ref, idec_ref,
                  cdec_ref, sdec_ref, chdec_ref, out_ref,
                  win_s, wpa_s, wpb_s, wout_s, stage, sem,
                  h_ref, xa_ref, lru_ref, ya_ref, yb_ref, r_ref, hst_ref,
                  *, final_norm):
    t = pl.program_id(1)
    T = TILE_T
    bf16 = jnp.bfloat16

    @pl.when((pl.program_id(0) == 0) & (t == 0))
    def _():
        _prepare_weights(win_hbm, wpa_hbm, wpb_hbm, wout_hbm, win_s, wpa_s, wpb_s, wout_s,
                         stage, sem)

    @pl.when(t == 0)
    def _():
        r_ref[...] = jnp.zeros_like(r_ref)
        hst_ref[...] = jnp.zeros_like(hst_ref)
        xa_ref[:, 0:HALO, :] = jnp.zeros((D_MODEL // LANES, HALO, LANES), jnp.float32)

    g_in = nin_ref[...]
    for r0 in range(0, T, ROW_CHUNK):
        xr = x_ref[0, r0:r0 + ROW_CHUNK, :]
        ms = jnp.mean(xr * xr, axis=-1, keepdims=True)
        h_ref[r0:r0 + ROW_CHUNK, :] = (xr * lax.rsqrt(ms + EPS) * g_in).astype(bf16)

    n_rg = T // SUBLANES
    sub = lax.broadcasted_iota(jnp.int32, (n_rg, SUBLANES, GROUP), 1)
    half = RET_DK // 2
    k_scale = RET_DK ** -0.5

    def project(g):
        return _dot(h_ref[...], win_s[:, g * GROUP_COLS:(g + 1) * GROUP_COLS])

    proj = project(0)
    for g in range(N_GROUPS):
        cs = slice(g * GROUP, (g + 1) * GROUP)
        kind = lambda i, p=proj: p[:, i * GROUP:(i + 1) * GROUP]

        xa = kind(KIND_XA)
        xc_blocks = []
        for b in range(GROUP // LANES):
            blk = g * (GROUP // LANES) + b
            lc = slice(blk * LANES, (blk + 1) * LANES)
            xa_b = xa[:, b * LANES:(b + 1) * LANES]
            xa_ref[blk, HALO:HALO + T, :] = xa_b
            xc_b = cb_ref[:, lc] + cw_ref[CONV_WIDTH - 1:CONV_WIDTH, lc] * xa_b
            for k in range(CONV_WIDTH - 1):
                lo = HALO - (CONV_WIDTH - 1) + k
                xc_b = xc_b + cw_ref[k:k + 1, lc] * xa_ref[blk, lo:lo + T, :]
            xa_ref[blk, 0:HALO, :] = xa_ref[blk, T:T + HALO, :]
            xc_blocks.append(xc_b)
        xc = jnp.concatenate(xc_blocks, axis=1)
        gates = _dot(xc.astype(bf16), pltpu.bitcast(gw_ref[g], bf16))

        q, k, v, gb = kind(KIND_Q), kind(KIND_K), kind(KIND_V), kind(KIND_GB)
        ga = kind(KIND_GA)
        cos = cos_ref[...]
        sin = sin_ref[...]
        q1, q2 = q[:, :half], q[:, half:]
        k1, k2 = k[:, :half], k[:, half:]
        qr = jnp.concatenate([q1 * cos - q2 * sin, q1 * sin + q2 * cos], axis=1).astype(bf16)
        kr = (jnp.concatenate([k1 * cos - k2 * sin, k1 * sin + k2 * cos], axis=1)
              * k_scale).astype(bf16)
        v_bf = v.astype(bf16)
        scores, kv = [], []
        for c in range(N_CHUNKS):
            rows = slice(c * CHUNK, (c + 1) * CHUNK)
            scores.append(lax.dot_general(qr[rows], kr[rows], (((1,), (1,)), ((), ())),
                                          preferred_element_type=jnp.float32))
            kv.append(lax.dot_general(kr[rows], (v[rows] * sdec_ref[:, cs]).astype(bf16),
                                      (((0,), (0,)), ((), ())),
                                      preferred_element_type=jnp.float32))

        if g + 1 < N_GROUPS:
            proj = project(g + 1)

        ti1 = 1.0 + jnp.tanh(gates[:, :GROUP] + 0.5 * gxb_ref[:, cs])
        tr1 = 1.0 + jnp.tanh(gates[:, GROUP:] + 0.5 * gab_ref[:, cs])
        nlam = -lam_ref[:, cs]
        softplus = jnp.maximum(nlam, 0.0) + jnp.log1p(jnp.exp(-jnp.abs(nlam)))
        nla = tr1 * ((0.5 * LRU_C) * softplus)
        a = jnp.exp(-nla)
        x4 = jnp.tanh(nla) * (a * (0.25 * a) + 0.25)
        mult = x4 * lax.rsqrt(jnp.maximum(x4, TINY))
        u = mult * (ti1 * xc)

        a3 = a.reshape(n_rg, SUBLANES, GROUP)
        u3 = u.reshape(n_rg, SUBLANES, GROUP)
        for s in (1, 2, 4):
            valid = sub >= s
            u_sh = jnp.where(valid, pltpu.roll(u3, s, 1), 0.0)
            a_sh = jnp.where(valid, pltpu.roll(a3, s, 1), 1.0)
            u3 = u3 + a3 * u_sh
            a3 = a3 * a_sh
        carry = hst_ref[:, cs]
        for j in range(n_rg):
            hj = u3[j] + a3[j] * carry
            lru_ref[j * SUBLANES:(j + 1) * SUBLANES, :] = hj
            carry = jnp.broadcast_to(hj[SUBLANES - 1:SUBLANES, :], (SUBLANES, GROUP))
        hst_ref[:, cs] = carry
        ya_ref[:, cs] = (ga * (1.0 + jnp.tanh(ga)) * lru_ref[...]).astype(bf16)

        silu_gb = gb * (1.0 + jnp.tanh(gb))
        gain = gn_ref[:, cs]
        r_state = r_ref[g]
        for c in range(N_CHUNKS):
            rows = slice(c * CHUNK, (c + 1) * CHUNK)
            cross = _dot(qr[rows], r_state.astype(bf16)) * cdec_ref[:, cs]
            r_state = chdec_ref[:, cs] * r_state + kv[c]
            inner = _dot((scores[c] * idec_ref[g]).astype(bf16), v_bf[rows])
            y = inner + cross
            mu = jnp.mean(y, axis=-1, keepdims=True)
            yc = y - mu
            var = jnp.mean(yc * yc, axis=-1, keepdims=True)
            gn = yc * lax.rsqrt(var + EPS) * gain
            yb_ref[rows, cs] = (silu_gb[rows] * gn).astype(bf16)
        r_ref[g] = r_state

    oa = _dot(ya_ref[...], wpa_s[...])
    ob = _dot(yb_ref[...], wpb_s[...])
    mg = _dot(h_ref[...], win_s[:, MERGE_COL0:MERGE_COL0 + 2 * D_MODEL])
    merged = ((1.0 + jnp.tanh(mg[:, :D_MODEL])) * oa
              + (1.0 + jnp.tanh(mg[:, D_MODEL:])) * ob).astype(bf16)
    out_ref[0] = _dot(merged, wout_s[...])
    g_fin = nfin_ref[...]
    for r0 in range(0, T, ROW_CHUNK):
        xo = x_ref[0, r0:r0 + ROW_CHUNK, :] + out_ref[0, r0:r0 + ROW_CHUNK, :]
        if final_norm:
            ms = jnp.mean(xo * xo, axis=-1, keepdims=True)
            xo = xo * lax.rsqrt(ms + EPS) * g_fin
        out_ref[0, r0:r0 + ROW_CHUNK, :] = xo


def _block_diag_groups(w):
    per = GROUP // LRU_BW
    w4 = w.reshape(N_GROUPS, per, LRU_BW, LRU_BW)
    eye = jnp.eye(per, dtype=w.dtype)
    return jnp.einsum('gikn,ij->gikjn', w4, eye).reshape(N_GROUPS, GROUP, GROUP)


def _pack_rows(w):
    *lead, k, n = w.shape
    pairs = w.astype(jnp.bfloat16).reshape(*lead, k // 2, 2, n)
    return lax.bitcast_convert_type(jnp.swapaxes(pairs, -1, -2), jnp.uint32)


def _retention_tables(seq):
    f32 = jnp.float32
    half = RET_DK // 2
    freqs = ROPE_THETA ** (-jnp.arange(half, dtype=f32) / half)
    ang = jnp.arange(seq, dtype=f32)[:, None] * freqs[None, :]
    log_g = jnp.log1p(-(2.0 ** (-5.0 - jnp.arange(RET_HEADS, dtype=f32))))
    idx = jnp.arange(CHUNK, dtype=f32)
    diff = idx[:, None] - idx[None, :]
    inner = jnp.where(diff >= 0, jnp.exp(jnp.maximum(diff, 0.0)[None] * log_g[:, None, None]), 0.0)
    cross = jnp.exp((idx[:, None] + 1.0) * log_g[None, :])
    state = jnp.exp((CHUNK - 1.0 - idx[:, None]) * log_g[None, :])
    chunk = jnp.exp(CHUNK * log_g)[None, :]
    rep = lambda m: jnp.repeat(m, RET_DV, axis=1)
    return jnp.cos(ang), jnp.sin(ang), inner, rep(cross), rep(state), rep(chunk)


def _const_spec(shape):
    zeros = (0,) * len(shape)
    return pl.BlockSpec(shape, lambda b, t: zeros, pipeline_mode=pl.Buffered(1))


def _layer(x, nin, win, cw, cb, gxw, gxb, gaw, gab, lam, gn, wpa, wpb, wout, nfin, tables,
           final_norm):
    B, S, D = x.shape
    assert D == D_MODEL and S % TILE_T == 0
    assert win.shape == (D, N_KINDS * D) and wpa.shape == wpb.shape == wout.shape == (D, D)
    bf16 = jnp.bfloat16
    row = lambda v: v.reshape(1, -1)
    gw = _pack_rows(0.5 * jnp.concatenate([_block_diag_groups(gxw), _block_diag_groups(gaw)], axis=-1))
    cos, sin, idec, cdec, sdec, chdec = tables
    operands = (
        x, row(nin), win, cw, row(cb), gw, row(gxb), row(gab), row(lam), row(gn),
        wpa, wpb, wout, row(nfin), cos, sin, idec, cdec, sdec, chdec)
    hbm_operands = (2, 10, 11, 12)
    tile_spec = pl.BlockSpec((1, TILE_T, D), lambda b, t: (b, t, 0))
    pos_spec = pl.BlockSpec((TILE_T, RET_DK // 2), lambda b, t: (t, 0))
    in_specs = []
    for i, o in enumerate(operands):
        if i == 0:
            in_specs.append(tile_spec)
        elif i in hbm_operands:
            in_specs.append(pl.BlockSpec(memory_space=pl.ANY))
        elif i in (14, 15):
            in_specs.append(pos_spec)
        else:
            in_specs.append(_const_spec(o.shape))
    scratch = [
        pltpu.VMEM((D, N_KINDS * D), bf16),
        pltpu.VMEM((D, D), bf16),
        pltpu.VMEM((D, D), bf16),
        pltpu.VMEM((D, D), bf16),
        pltpu.VMEM((2, D, STAGE_COLS), jnp.float32),
        pltpu.SemaphoreType.DMA((2,)),
        pltpu.VMEM((TILE_T, D), bf16),
        pltpu.VMEM((D // LANES, TILE_T + HALO, LANES), jnp.float32),
        pltpu.VMEM((TILE_T, GROUP), jnp.float32),
        pltpu.VMEM((TILE_T, D), bf16),
        pltpu.VMEM((TILE_T, D), bf16),
        pltpu.VMEM((RET_HEADS, RET_DK, RET_DV), jnp.float32),
        pltpu.VMEM((SUBLANES, D), jnp.float32),
    ]
    return pl.pallas_call(
        functools.partial(_block_kernel, final_norm=final_norm),
        grid=(B, S // TILE_T),
        in_specs=in_specs,
        out_specs=tile_spec,
        out_shape=jax.ShapeDtypeStruct(x.shape, x.dtype),
        scratch_shapes=scratch,
        compiler_params=pltpu.CompilerParams(
            dimension_semantics=("arbitrary", "arbitrary"),
            vmem_limit_bytes=VMEM_LIMIT_BYTES),
        name="hybrid_block",
    )(*operands)


def kernel(x, norm_in, w_in, conv_w, conv_b, gate_x_w, gate_x_b, gate_a_w, gate_a_b, lru_lambda, gn_gain, w_proj_a, w_proj_b, w_out, norm_final):
    depth = w_in.shape[0]
    tables = _retention_tables(x.shape[1])
    for l in range(depth):
        x = _layer(x, norm_in[l], w_in[l], conv_w[l], conv_b[l], gate_x_w[l], gate_x_b[l],
                   gate_a_w[l], gate_a_b[l], lru_lambda[l], gn_gain[l].reshape(-1),
                   w_proj_a[l], w_proj_b[l], w_out[l], norm_final, tables,
                   final_norm=(l == depth - 1))
    return x
```

```python
import functools

import jax
import jax.numpy as jnp
from jax import lax
from jax.experimental import pallas as pl
from jax.experimental.pallas import tpu as pltpu

D_MODEL = 1024
LRU_BLOCKS = 16
LRU_BW = D_MODEL // LRU_BLOCKS
CONV_WIDTH = 4
LRU_C = 8.0
RET_HEADS = 4
RET_DK = 256
RET_DV = 256
CHUNK = 256
ROPE_THETA = 10000.0
EPS = 1e-6

KIND_XA, KIND_GA, KIND_Q, KIND_K, KIND_V, KIND_GB, KIND_MA, KIND_MB = range(8)
N_KINDS = 8
N_GROUP_KINDS = 6
HALVED_KINDS = (KIND_GA, KIND_GB, KIND_MA, KIND_MB)

LANES = 128
SUBLANES = 8
GROUP = 256
N_GROUPS = D_MODEL // GROUP
GROUP_COLS = N_GROUP_KINDS * GROUP
MERGE_COL0 = N_GROUPS * GROUP_COLS
TILE_T = 512
N_CHUNKS = TILE_T // CHUNK
ROW_CHUNK = 32
STAGE_COLS = 512
PREP_ROWS = 64
HALO = SUBLANES
VMEM_LIMIT_BYTES = 56 * 1024 * 1024
TINY = 1e-37


def _dot(a, b):
    return jnp.dot(a, b, preferred_element_type=jnp.float32)


def _prepare_weights(win_hbm, wpa_hbm, wpb_hbm, wout_hbm, win_s, wpa_s, wpb_s, wout_s,
                     stage, sem):
    slabs = []
    for kind in range(N_KINDS):
        for c0 in range(0, D_MODEL, STAGE_COLS):
            src = win_hbm.at[:, pl.ds(kind * D_MODEL + c0, STAGE_COLS)]
            if kind < N_GROUP_KINDS:
                pieces = [(c - c0, (c // GROUP) * GROUP_COLS + kind * GROUP, GROUP)
                          for c in range(c0, c0 + STAGE_COLS, GROUP)]
            else:
                pieces = [(0, kind * D_MODEL + c0, STAGE_COLS)]
            slabs.append((src, win_s, pieces, kind in HALVED_KINDS))
    for src_hbm, dst, halve in ((wpa_hbm, wpa_s, True), (wpb_hbm, wpb_s, True),
                                (wout_hbm, wout_s, False)):
        for c0 in range(0, D_MODEL, STAGE_COLS):
            slabs.append((src_hbm.at[:, pl.ds(c0, STAGE_COLS)], dst,
                          [(0, c0, STAGE_COLS)], halve))

    def copy(n):
        return pltpu.make_async_copy(slabs[n][0], stage.at[n % 2], sem.at[n % 2])

    copy(0).start()
    for n, (_, dst, pieces, halve) in enumerate(slabs):
        if n + 1 < len(slabs):
            copy(n + 1).start()
        copy(n).wait()

        def convert(i, carry, slot=n % 2, dst=dst, pieces=pieces, halve=halve):
            rows = pl.ds(pl.multiple_of(i * PREP_ROWS, PREP_ROWS), PREP_ROWS)
            w = stage[slot, rows, :]
            if halve:
                w = 0.5 * w
            w = w.astype(jnp.bfloat16)
            for s0, d0, width in pieces:
                dst[rows, d0:d0 + width] = w[:, s0:s0 + width]
            return carry

        lax.fori_loop(0, D_MODEL // PREP_ROWS, convert, 0)


def _block_kernel(x_ref, nin_ref, win_hbm, cw_ref, cb_ref, gw_ref, gxb_ref, gab_ref, lam_ref,
                  gn_ref, wpa_hbm, wpb_hbm, wout_hbm, nfin_ref, cos_ref, sin_ref, idec_ref,
                  cdec_ref, sdec_ref, chdec_ref, out_ref,
                  win_s, wpa_s, wpb_s, wout_s, stage, sem,
                  h_ref, xa_ref, lru_ref, ya_ref, yb_ref, r_ref, hst_ref,
                  *, final_norm):
    t = pl.program_id(1)
    T = TILE_T
    bf16 = jnp.bfloat16

    @pl.when((pl.program_id(0) == 0) & (t == 0))
    def _():
        _prepare_weights(win_hbm, wpa_hbm, wpb_hbm, wout_hbm, win_s, wpa_s, wpb_s, wout_s,
                         stage, sem)

    @pl.when(t == 0)
    def _():
        r_ref[...] = jnp.zeros_like(r_ref)
        hst_ref[...] = jnp.zeros_like(hst_ref)
        xa_ref[:, 0:HALO, :] = jnp.zeros((D_MODEL // LANES, HALO, LANES), jnp.float32)

    g_in = nin_ref[...]
    for r0 in range(0, T, ROW_CHUNK):
        xr = x_ref[0, r0:r0 + ROW_CHUNK, :]
        ms = jnp.mean(xr * xr, axis=-1, keepdims=True)
        h_ref[r0:r0 + ROW_CHUNK, :] = (xr * lax.rsqrt(ms + EPS) * g_in).astype(bf16)

    n_rg = T // SUBLANES
    sub = lax.broadcasted_iota(jnp.int32, (n_rg, SUBLANES, GROUP), 1)
    half = RET_DK // 2
    k_scale = RET_DK ** -0.5

    def project(g):
        return _dot(h_ref[...], win_s[:, g * GROUP_COLS:(g + 1) * GROUP_COLS])

    proj = project(0)
    for g in range(N_GROUPS):
        cs = slice(g * GROUP, (g + 1) * GROUP)
        kind = lambda i, p=proj: p[:, i * GROUP:(i + 1) * GROUP]

        xa = kind(KIND_XA)
        xc_blocks = []
        for b in range(GROUP // LANES):
            blk = g * (GROUP // LANES) + b
            lc = slice(blk * LANES, (blk + 1) * LANES)
            xa_b = xa[:, b * LANES:(b + 1) * LANES]
            xa_ref[blk, HALO:HALO + T, :] = xa_b
            xc_b = cb_ref[:, lc] + cw_ref[CONV_WIDTH - 1:CONV_WIDTH, lc] * xa_b
            for k in range(CONV_WIDTH - 1):
                lo = HALO - (CONV_WIDTH - 1) + k
                xc_b = xc_b + cw_ref[k:k + 1, lc] * xa_ref[blk, lo:lo + T, :]
            xa_ref[blk, 0:HALO, :] = xa_ref[blk, T:T + HALO, :]
            xc_blocks.append(xc_b)
        xc = jnp.concatenate(xc_blocks, axis=1)
        gates = _dot(xc.astype(bf16), pltpu.bitcast(gw_ref[g], bf16))

        q, k, v, gb = kind(KIND_Q), kind(KIND_K), kind(KIND_V), kind(KIND_GB)
        ga = kind(KIND_GA)
        pos = pl.ds(pl.multiple_of(t * T, T), T)
        cos = cos_ref[pos, :]
        sin = sin_ref[pos, :]
        q1, q2 = q[:, :half], q[:, half:]
        k1, k2 = k[:, :half], k[:, half:]
        qr = jnp.concatenate([q1 * cos - q2 * sin, q1 * sin + q2 * cos], axis=1).astype(bf16)
        kr = (jnp.concatenate([k1 * cos - k2 * sin, k1 * sin + k2 * cos], axis=1)
              * k_scale).astype(bf16)
        v_bf = v.astype(bf16)
        scores, kv = [], []
        for c in range(N_CHUNKS):
            rows = slice(c * CHUNK, (c + 1) * CHUNK)
            scores.append(lax.dot_general(qr[rows], kr[rows], (((1,), (1,)), ((), ())),
                                          preferred_element_type=jnp.float32))
            kv.append(lax.dot_general(kr[rows], (v[rows] * sdec_ref[:, cs]).astype(bf16),
                                      (((0,), (0,)), ((), ())),
                                      preferred_element_type=jnp.float32))

        if g + 1 < N_GROUPS:
            proj = project(g + 1)

        ti1 = 1.0 + jnp.tanh(gates[:, :GROUP] + 0.5 * gxb_ref[:, cs])
        tr1 = 1.0 + jnp.tanh(gates[:, GROUP:] + 0.5 * gab_ref[:, cs])
        nlam = -lam_ref[:, cs]
        softplus = jnp.maximum(nlam, 0.0) + jnp.log1p(jnp.exp(-jnp.abs(nlam)))
        nla = tr1 * ((0.5 * LRU_C) * softplus)
        a = jnp.exp(-nla)
        x4 = jnp.tanh(nla) * (a * (0.25 * a) + 0.25)
        mult = x4 * lax.rsqrt(jnp.maximum(x4, TINY))
        u = mult * (ti1 * xc)

        a3 = a.reshape(n_rg, SUBLANES, GROUP)
        u3 = u.reshape(n_rg, SUBLANES, GROUP)
        for s in (1, 2, 4):
            valid = sub >= s
            u_sh = jnp.where(valid, pltpu.roll(u3, s, 1), 0.0)
            a_sh = jnp.where(valid, pltpu.roll(a3, s, 1), 1.0)
            u3 = u3 + a3 * u_sh
            a3 = a3 * a_sh
        carry = hst_ref[:, cs]
        for j in range(n_rg):
            hj = u3[j] + a3[j] * carry
            lru_ref[j * SUBLANES:(j + 1) * SUBLANES, :] = hj
            carry = jnp.broadcast_to(hj[SUBLANES - 1:SUBLANES, :], (SUBLANES, GROUP))
        hst_ref[:, cs] = carry
        ya_ref[:, cs] = (ga * (1.0 + jnp.tanh(ga)) * lru_ref[...]).astype(bf16)

        silu_gb = gb * (1.0 + jnp.tanh(gb))
        gain = gn_ref[:, cs]
        r_state = r_ref[g]
        for c in range(N_CHUNKS):
            rows = slice(c * CHUNK, (c + 1) * CHUNK)
            cross = _dot(qr[rows], r_state.astype(bf16)) * cdec_ref[:, cs]
            r_state = chdec_ref[:, cs] * r_state + kv[c]
            inner = _dot((scores[c] * idec_ref[g]).astype(bf16), v_bf[rows])
            y = inner + cross
            mu = jnp.mean(y, axis=-1, keepdims=True)
            yc = y - mu
            var = jnp.mean(yc * yc, axis=-1, keepdims=True)
            gn = yc * lax.rsqrt(var + EPS) * gain
            yb_ref[rows, cs] = (silu_gb[rows] * gn).astype(bf16)
        r_ref[g] = r_state

    oa = _dot(ya_ref[...], wpa_s[...])
    ob = _dot(yb_ref[...], wpb_s[...])
    mg = _dot(h_ref[...], win_s[:, MERGE_COL0:MERGE_COL0 + 2 * D_MODEL])
    merged = ((1.0 + jnp.tanh(mg[:, :D_MODEL])) * oa
              + (1.0 + jnp.tanh(mg[:, D_MODEL:])) * ob).astype(bf16)
    out_ref[0] = _dot(merged, wout_s[...])
    g_fin = nfin_ref[...]
    for r0 in range(0, T, ROW_CHUNK):
        xo = x_ref[0, r0:r0 + ROW_CHUNK, :] + out_ref[0, r0:r0 + ROW_CHUNK, :]
        if final_norm:
            ms = jnp.mean(xo * xo, axis=-1, keepdims=True)
            xo = xo * lax.rsqrt(ms + EPS) * g_fin
        out_ref[0, r0:r0 + ROW_CHUNK, :] = xo


def _block_diag_groups(w):
    per = GROUP // LRU_BW
    w4 = w.reshape(N_GROUPS, per, LRU_BW, LRU_BW)
    eye = jnp.eye(per, dtype=w.dtype)
    return jnp.einsum('gikn,ij->gikjn', w4, eye).reshape(N_GROUPS, GROUP, GROUP)


def _pack_rows(w):
    *lead, k, n = w.shape
    pairs = w.astype(jnp.bfloat16).reshape(*lead, k // 2, 2, n)
    return lax.bitcast_convert_type(jnp.swapaxes(pairs, -1, -2), jnp.uint32)


def _retention_tables(seq):
    f32 = jnp.float32
    half = RET_DK // 2
    freqs = ROPE_THETA ** (-jnp.arange(half, dtype=f32) / half)
    ang = jnp.arange(seq, dtype=f32)[:, None] * freqs[None, :]
    log_g = jnp.log1p(-(2.0 ** (-5.0 - jnp.arange(RET_HEADS, dtype=f32))))
    idx = jnp.arange(CHUNK, dtype=f32)
    diff = idx[:, None] - idx[None, :]
    inner = jnp.where(diff >= 0, jnp.exp(jnp.maximum(diff, 0.0)[None] * log_g[:, None, None]), 0.0)
    cross = jnp.exp((idx[:, None] + 1.0) * log_g[None, :])
    state = jnp.exp((CHUNK - 1.0 - idx[:, None]) * log_g[None, :])
    chunk = jnp.exp(CHUNK * log_g)[None, :]
    rep = lambda m: jnp.repeat(m, RET_DV, axis=1)
    return jnp.cos(ang), jnp.sin(ang), inner, rep(cross), rep(state), rep(chunk)


def _const_spec(shape):
    zeros = (0,) * len(shape)
    return pl.BlockSpec(shape, lambda b, t: zeros, pipeline_mode=pl.Buffered(1))


def _layer(x, nin, win, cw, cb, gxw, gxb, gaw, gab, lam, gn, wpa, wpb, wout, nfin, tables,
           final_norm):
    B, S, D = x.shape
    assert D == D_MODEL and S % TILE_T == 0
    assert win.shape == (D, N_KINDS * D) and wpa.shape == wpb.shape == wout.shape == (D, D)
    bf16 = jnp.bfloat16
    row = lambda v: v.reshape(1, -1)
    gw = _pack_rows(0.5 * jnp.concatenate([_block_diag_groups(gxw), _block_diag_groups(gaw)], axis=-1))
    cos, sin, idec, cdec, sdec, chdec = tables
    operands = (
        x, row(nin), win, cw, row(cb), gw, row(gxb), row(gab), row(lam), row(gn),
        wpa, wpb, wout, row(nfin), cos, sin, idec, cdec, sdec, chdec)
    hbm_operands = (2, 10, 11, 12)
    tile_spec = pl.BlockSpec((1, TILE_T, D), lambda b, t: (b, t, 0))
    in_specs = []
    for i, o in enumerate(operands):
        if i == 0:
            in_specs.append(tile_spec)
        elif i in hbm_operands:
            in_specs.append(pl.BlockSpec(memory_space=pl.ANY))
        else:
            in_specs.append(_const_spec(o.shape))
    scratch = [
        pltpu.VMEM((D, N_KINDS * D), bf16),
        pltpu.VMEM((D, D), bf16),
        pltpu.VMEM((D, D), bf16),
        pltpu.VMEM((D, D), bf16),
        pltpu.VMEM((2, D, STAGE_COLS), jnp.float32),
        pltpu.SemaphoreType.DMA((2,)),
        pltpu.VMEM((TILE_T, D), bf16),
        pltpu.VMEM((D // LANES, TILE_T + HALO, LANES), jnp.float32),
        pltpu.VMEM((TILE_T, GROUP), jnp.float32),
        pltpu.VMEM((TILE_T, D), bf16),
        pltpu.VMEM((TILE_T, D), bf16),
        pltpu.VMEM((RET_HEADS, RET_DK, RET_DV), jnp.float32),
        pltpu.VMEM((SUBLANES, D), jnp.float32),
    ]
    return pl.pallas_call(
        functools.partial(_block_kernel, final_norm=final_norm),
        grid=(B, S // TILE_T),
        in_specs=in_specs,
        out_specs=tile_spec,
        out_shape=jax.ShapeDtypeStruct(x.shape, x.dtype),
        scratch_shapes=scratch,
        compiler_params=pltpu.CompilerParams(
            dimension_semantics=("arbitrary", "arbitrary"),
            vmem_limit_bytes=VMEM_LIMIT_BYTES),
        name="hybrid_block",
    )(*operands)


def kernel(x, norm_in, w_in, conv_w, conv_b, gate_x_w, gate_x_b, gate_a_w, gate_a_b, lru_lambda, gn_gain, w_proj_a, w_proj_b, w_out, norm_final):
    depth = w_in.shape[0]
    tables = _retention_tables(x.shape[1])
    for l in range(depth):
        x = _layer(x, norm_in[l], w_in[l], conv_w[l], conv_b[l], gate_x_w[l], gate_x_b[l],
                   gate_a_w[l], gate_a_b[l], lru_lambda[l], gn_gain[l].reshape(-1),
                   w_proj_a[l], w_proj_b[l], w_out[l], norm_final, tables,
                   final_norm=(l == depth - 1))
    return x
```

```python
import functools

import jax
import jax.numpy as jnp
from jax import lax
from jax.experimental import pallas as pl
from jax.experimental.pallas import tpu as pltpu

D_MODEL = 1024
LRU_BLOCKS = 16
LRU_BW = D_MODEL // LRU_BLOCKS
CONV_WIDTH = 4
LRU_C = 8.0
RET_HEADS = 4
RET_DK = 256
RET_DV = 256
CHUNK = 256
ROPE_THETA = 10000.0
EPS = 1e-6

KIND_XA, KIND_GA, KIND_Q, KIND_K, KIND_V, KIND_GB, KIND_MA, KIND_MB = range(8)
N_KINDS = 8
N_GROUP_KINDS = 6
HALVED_KINDS = (KIND_GA, KIND_GB, KIND_MA, KIND_MB)

LANES = 128
SUBLANES = 8
GROUP = 256
N_GROUPS = D_MODEL // GROUP
GROUP_COLS = N_GROUP_KINDS * GROUP
MERGE_COL0 = N_GROUPS * GROUP_COLS
TILE_T = 256
TILES_PER_STEP = 4
STEP_T = TILE_T * TILES_PER_STEP
N_CHUNKS = TILE_T // CHUNK
ROW_CHUNK = 32
STAGE_COLS = 512
PREP_ROWS = 64
HALO = SUBLANES
VMEM_LIMIT_BYTES = 56 * 1024 * 1024
TINY = 1e-37


def _dot(a, b):
    return jnp.dot(a, b, preferred_element_type=jnp.float32)


def _prepare_weights(win_hbm, wpa_hbm, wpb_hbm, wout_hbm, win_s, wpa_s, wpb_s, wout_s,
                     stage, sem):
    slabs = []
    for kind in range(N_KINDS):
        for c0 in range(0, D_MODEL, STAGE_COLS):
            src = win_hbm.at[:, pl.ds(kind * D_MODEL + c0, STAGE_COLS)]
            if kind < N_GROUP_KINDS:
                pieces = [(c - c0, (c // GROUP) * GROUP_COLS + kind * GROUP, GROUP)
                          for c in range(c0, c0 + STAGE_COLS, GROUP)]
            else:
                pieces = [(0, kind * D_MODEL + c0, STAGE_COLS)]
            slabs.append((src, win_s, pieces, kind in HALVED_KINDS))
    for src_hbm, dst, halve in ((wpa_hbm, wpa_s, True), (wpb_hbm, wpb_s, True),
                                (wout_hbm, wout_s, False)):
        for c0 in range(0, D_MODEL, STAGE_COLS):
            slabs.append((src_hbm.at[:, pl.ds(c0, STAGE_COLS)], dst,
                          [(0, c0, STAGE_COLS)], halve))

    def copy(n):
        return pltpu.make_async_copy(slabs[n][0], stage.at[n % 2], sem.at[n % 2])

    copy(0).start()
    for n, (_, dst, pieces, halve) in enumerate(slabs):
        if n + 1 < len(slabs):
            copy(n + 1).start()
        copy(n).wait()

        def convert(i, carry, slot=n % 2, dst=dst, pieces=pieces, halve=halve):
            rows = pl.ds(pl.multiple_of(i * PREP_ROWS, PREP_ROWS), PREP_ROWS)
            w = stage[slot, rows, :]
            if halve:
                w = 0.5 * w
            w = w.astype(jnp.bfloat16)
            for s0, d0, width in pieces:
                dst[rows, d0:d0 + width] = w[:, s0:s0 + width]
            return carry

        lax.fori_loop(0, D_MODEL // PREP_ROWS, convert, 0)


def _block_kernel(x_ref, nin_ref, win_hbm, cw_ref, cb_ref, gw_ref, gxb_ref, gab_ref, lam_ref,
                  gn_ref, wpa_hbm, wpb_hbm, wout_hbm, nfin_ref, cos_ref, sin_ref, idec_ref,
                  cdec_ref, sdec_ref, chdec_ref, out_ref,
                  win_s, wpa_s, wpb_s, wout_s, stage, sem,
                  h_ref, xa_ref, lru_ref, ya_ref, yb_ref, r_ref, hst_ref,
                  *, final_norm):
    @pl.when((pl.program_id(0) == 0) & (pl.program_id(1) == 0))
    def _():
        _prepare_weights(win_hbm, wpa_hbm, wpb_hbm, wout_hbm, win_s, wpa_s, wpb_s, wout_s,
                         stage, sem)

    def tile(i, carry):
        _tile_body(pl.program_id(1) * TILES_PER_STEP + i, pl.multiple_of(i * TILE_T, TILE_T),
                   x_ref, nin_ref, cw_ref, cb_ref, gw_ref, gxb_ref, gab_ref, lam_ref, gn_ref,
                   nfin_ref, cos_ref, sin_ref, idec_ref, cdec_ref, sdec_ref, chdec_ref, out_ref,
                   win_s, wpa_s, wpb_s, wout_s, h_ref, xa_ref, lru_ref, ya_ref, yb_ref, r_ref,
                   hst_ref, final_norm)
        return carry

    lax.fori_loop(0, TILES_PER_STEP, tile, 0)


def _tile_body(t, row0, x_ref, nin_ref, cw_ref, cb_ref, gw_ref, gxb_ref, gab_ref, lam_ref, gn_ref,
               nfin_ref, cos_ref, sin_ref, idec_ref, cdec_ref, sdec_ref, chdec_ref, out_ref,
               win_s, wpa_s, wpb_s, wout_s, h_ref, xa_ref, lru_ref, ya_ref, yb_ref, r_ref,
               hst_ref, final_norm):
    T = TILE_T
    bf16 = jnp.bfloat16

    @pl.when(t == 0)
    def _():
        r_ref[...] = jnp.zeros_like(r_ref)
        hst_ref[...] = jnp.zeros_like(hst_ref)
        xa_ref[:, 0:HALO, :] = jnp.zeros((D_MODEL // LANES, HALO, LANES), jnp.float32)

    g_in = nin_ref[...]
    for r0 in range(0, T, ROW_CHUNK):
        xr = x_ref[0, pl.ds(row0 + r0, ROW_CHUNK), :]
        ms = jnp.mean(xr * xr, axis=-1, keepdims=True)
        h_ref[r0:r0 + ROW_CHUNK, :] = (xr * lax.rsqrt(ms + EPS) * g_in).astype(bf16)

    n_rg = T // SUBLANES
    sub = lax.broadcasted_iota(jnp.int32, (n_rg, SUBLANES, GROUP), 1)
    half = RET_DK // 2
    k_scale = RET_DK ** -0.5

    def project(g):
        return _dot(h_ref[...], win_s[:, g * GROUP_COLS:(g + 1) * GROUP_COLS])

    proj = project(0)
    for g in range(N_GROUPS):
        cs = slice(g * GROUP, (g + 1) * GROUP)
        kind = lambda i, p=proj: p[:, i * GROUP:(i + 1) * GROUP]

        xa = kind(KIND_XA)
        xc_blocks = []
        for b in range(GROUP // LANES):
            blk = g * (GROUP // LANES) + b
            lc = slice(blk * LANES, (blk + 1) * LANES)
            xa_b = xa[:, b * LANES:(b + 1) * LANES]
            xa_ref[blk, HALO:HALO + T, :] = xa_b
            xc_b = cb_ref[:, lc] + cw_ref[CONV_WIDTH - 1:CONV_WIDTH, lc] * xa_b
            for k in range(CONV_WIDTH - 1):
                lo = HALO - (CONV_WIDTH - 1) + k
                xc_b = xc_b + cw_ref[k:k + 1, lc] * xa_ref[blk, lo:lo + T, :]
            xa_ref[blk, 0:HALO, :] = xa_ref[blk, T:T + HALO, :]
            xc_blocks.append(xc_b)
        xc = jnp.concatenate(xc_blocks, axis=1)
        gates = _dot(xc.astype(bf16), pltpu.bitcast(gw_ref[g], bf16))

        q, k, v, gb = kind(KIND_Q), kind(KIND_K), kind(KIND_V), kind(KIND_GB)
        ga = kind(KIND_GA)
        pos = pl.ds(pl.multiple_of(t * T, T), T)
        cos = cos_ref[pos, :]
        sin = sin_ref[pos, :]
        q1, q2 = q[:, :half], q[:, half:]
        k1, k2 = k[:, :half], k[:, half:]
        qr = jnp.concatenate([q1 * cos - q2 * sin, q1 * sin + q2 * cos], axis=1).astype(bf16)
        kr = (jnp.concatenate([k1 * cos - k2 * sin, k1 * sin + k2 * cos], axis=1)
              * k_scale).astype(bf16)
        v_bf = v.astype(bf16)
        scores, kv = [], []
        for c in range(N_CHUNKS):
            rows = slice(c * CHUNK, (c + 1) * CHUNK)
            scores.append(lax.dot_general(qr[rows], kr[rows], (((1,), (1,)), ((), ())),
                                          preferred_element_type=jnp.float32))
            kv.append(lax.dot_general(kr[rows], (v[rows] * sdec_ref[:, cs]).astype(bf16),
                                      (((0,), (0,)), ((), ())),
                                      preferred_element_type=jnp.float32))

        if g + 1 < N_GROUPS:
            proj = project(g + 1)

        ti1 = 1.0 + jnp.tanh(gates[:, :GROUP] + 0.5 * gxb_ref[:, cs])
        tr1 = 1.0 + jnp.tanh(gates[:, GROUP:] + 0.5 * gab_ref[:, cs])
        nlam = -lam_ref[:, cs]
        softplus = jnp.maximum(nlam, 0.0) + jnp.log1p(jnp.exp(-jnp.abs(nlam)))
        nla = tr1 * ((0.5 * LRU_C) * softplus)
        a = jnp.exp(-nla)
        x4 = jnp.tanh(nla) * (a * (0.25 * a) + 0.25)
        mult = x4 * lax.rsqrt(jnp.maximum(x4, TINY))
        u = mult * (ti1 * xc)

        a3 = a.reshape(n_rg, SUBLANES, GROUP)
        u3 = u.reshape(n_rg, SUBLANES, GROUP)
        for s in (1, 2, 4):
            valid = sub >= s
            u_sh = jnp.where(valid, pltpu.roll(u3, s, 1), 0.0)
            a_sh = jnp.where(valid, pltpu.roll(a3, s, 1), 1.0)
            u3 = u3 + a3 * u_sh
            a3 = a3 * a_sh
        carry = hst_ref[:, cs]
        for j in range(n_rg):
            hj = u3[j] + a3[j] * carry
            lru_ref[j * SUBLANES:(j + 1) * SUBLANES, :] = hj
            carry = jnp.broadcast_to(hj[SUBLANES - 1:SUBLANES, :], (SUBLANES, GROUP))
        hst_ref[:, cs] = carry
        ya_ref[:, cs] = (ga * (1.0 + jnp.tanh(ga)) * lru_ref[...]).astype(bf16)

        silu_gb = gb * (1.0 + jnp.tanh(gb))
        gain = gn_ref[:, cs]
        r_state = r_ref[g]
        for c in range(N_CHUNKS):
            rows = slice(c * CHUNK, (c + 1) * CHUNK)
            cross = _dot(qr[rows], r_state.astype(bf16)) * cdec_ref[:, cs]
            r_state = chdec_ref[:, cs] * r_state + kv[c]
            inner = _dot((scores[c] * idec_ref[g]).astype(bf16), v_bf[rows])
            y = inner + cross
            mu = jnp.mean(y, axis=-1, keepdims=True)
            yc = y - mu
            var = jnp.mean(yc * yc, axis=-1, keepdims=True)
            gn = yc * lax.rsqrt(var + EPS) * gain
            yb_ref[rows, cs] = (silu_gb[rows] * gn).astype(bf16)
        r_ref[g] = r_state

    oa = _dot(ya_ref[...], wpa_s[...])
    ob = _dot(yb_ref[...], wpb_s[...])
    mg = _dot(h_ref[...], win_s[:, MERGE_COL0:MERGE_COL0 + 2 * D_MODEL])
    merged = ((1.0 + jnp.tanh(mg[:, :D_MODEL])) * oa
              + (1.0 + jnp.tanh(mg[:, D_MODEL:])) * ob).astype(bf16)
    out_ref[0, pl.ds(row0, T), :] = _dot(merged, wout_s[...])
    g_fin = nfin_ref[...]
    for r0 in range(0, T, ROW_CHUNK):
        rows = pl.ds(row0 + r0, ROW_CHUNK)
        xo = x_ref[0, rows, :] + out_ref[0, rows, :]
        if final_norm:
            ms = jnp.mean(xo * xo, axis=-1, keepdims=True)
            xo = xo * lax.rsqrt(ms + EPS) * g_fin
        out_ref[0, rows, :] = xo


def _block_diag_groups(w):
    per = GROUP // LRU_BW
    w4 = w.reshape(N_GROUPS, per, LRU_BW, LRU_BW)
    eye = jnp.eye(per, dtype=w.dtype)
    return jnp.einsum('gikn,ij->gikjn', w4, eye).reshape(N_GROUPS, GROUP, GROUP)


def _pack_rows(w):
    *lead, k, n = w.shape
    pairs = w.astype(jnp.bfloat16).reshape(*lead, k // 2, 2, n)
    return lax.bitcast_convert_type(jnp.swapaxes(pairs, -1, -2), jnp.uint32)


def _retention_tables(seq):
    f32 = jnp.float32
    half = RET_DK // 2
    freqs = ROPE_THETA ** (-jnp.arange(half, dtype=f32) / half)
    ang = jnp.arange(seq, dtype=f32)[:, None] * freqs[None, :]
    log_g = jnp.log1p(-(2.0 ** (-5.0 - jnp.arange(RET_HEADS, dtype=f32))))
    idx = jnp.arange(CHUNK, dtype=f32)
    diff = idx[:, None] - idx[None, :]
    inner = jnp.where(diff >= 0, jnp.exp(jnp.maximum(diff, 0.0)[None] * log_g[:, None, None]), 0.0)
    cross = jnp.exp((idx[:, None] + 1.0) * log_g[None, :])
    state = jnp.exp((CHUNK - 1.0 - idx[:, None]) * log_g[None, :])
    chunk = jnp.exp(CHUNK * log_g)[None, :]
    rep = lambda m: jnp.repeat(m, RET_DV, axis=1)
    return jnp.cos(ang), jnp.sin(ang), inner, rep(cross), rep(state), rep(chunk)


def _const_spec(shape):
    zeros = (0,) * len(shape)
    return pl.BlockSpec(shape, lambda b, t: zeros, pipeline_mode=pl.Buffered(1))


def _layer(x, nin, win, cw, cb, gxw, gxb, gaw, gab, lam, gn, wpa, wpb, wout, nfin, tables,
           final_norm):
    B, S, D = x.shape
    assert D == D_MODEL and S % STEP_T == 0
    assert win.shape == (D, N_KINDS * D) and wpa.shape == wpb.shape == wout.shape == (D, D)
    bf16 = jnp.bfloat16
    row = lambda v: v.reshape(1, -1)
    gw = _pack_rows(0.5 * jnp.concatenate([_block_diag_groups(gxw), _block_diag_groups(gaw)], axis=-1))
    cos, sin, idec, cdec, sdec, chdec = tables
    operands = (
        x, row(nin), win, cw, row(cb), gw, row(gxb), row(gab), row(lam), row(gn),
        wpa, wpb, wout, row(nfin), cos, sin, idec, cdec, sdec, chdec)
    hbm_operands = (2, 10, 11, 12)
    tile_spec = pl.BlockSpec((1, STEP_T, D), lambda b, t: (b, t, 0))
    in_specs = []
    for i, o in enumerate(operands):
        if i == 0:
            in_specs.append(tile_spec)
        elif i in hbm_operands:
            in_specs.append(pl.BlockSpec(memory_space=pl.ANY))
        else:
            in_specs.append(_const_spec(o.shape))
    scratch = [
        pltpu.VMEM((D, N_KINDS * D), bf16),
        pltpu.VMEM((D, D), bf16),
        pltpu.VMEM((D, D), bf16),
        pltpu.VMEM((D, D), bf16),
        pltpu.VMEM((2, D, STAGE_COLS), jnp.float32),
        pltpu.SemaphoreType.DMA((2,)),
        pltpu.VMEM((TILE_T, D), bf16),
        pltpu.VMEM((D // LANES, TILE_T + HALO, LANES), jnp.float32),
        pltpu.VMEM((TILE_T, GROUP), jnp.float32),
        pltpu.VMEM((TILE_T, D), bf16),
        pltpu.VMEM((TILE_T, D), bf16),
        pltpu.VMEM((RET_HEADS, RET_DK, RET_DV), jnp.float32),
        pltpu.VMEM((SUBLANES, D), jnp.float32),
    ]
    return pl.pallas_call(
        functools.partial(_block_kernel, final_norm=final_norm),
        grid=(B, S // STEP_T),
        in_specs=in_specs,
        out_specs=tile_spec,
        out_shape=jax.ShapeDtypeStruct(x.shape, x.dtype),
        scratch_shapes=scratch,
        compiler_params=pltpu.CompilerParams(
            dimension_semantics=("arbitrary", "arbitrary"),
            vmem_limit_bytes=VMEM_LIMIT_BYTES),
        name="hybrid_block",
    )(*operands)


def kernel(x, norm_in, w_in, conv_w, conv_b, gate_x_w, gate_x_b, gate_a_w, gate_a_b, lru_lambda, gn_gain, w_proj_a, w_proj_b, w_out, norm_final):
    depth = w_in.shape[0]
    tables = _retention_tables(x.shape[1])
    for l in range(depth):
        x = _layer(x, norm_in[l], w_in[l], conv_w[l], conv_b[l], gate_x_w[l], gate_x_b[l],
                   gate_a_w[l], gate_a_b[l], lru_lambda[l], gn_gain[l].reshape(-1),
                   w_proj_a[l], w_proj_b[l], w_out[l], norm_final, tables,
                   final_norm=(l == depth - 1))
    return x
```

```python
import functools

import jax
import jax.numpy as jnp
from jax import lax
from jax.experimental import pallas as pl
from jax.experimental.pallas import tpu as pltpu

D_MODEL = 1024
LRU_BLOCKS = 16
LRU_BW = D_MODEL // LRU_BLOCKS
CONV_WIDTH = 4
LRU_C = 8.0
RET_HEADS = 4
RET_DK = 256
RET_DV = 256
CHUNK = 256
ROPE_THETA = 10000.0
EPS = 1e-6

KIND_XA, KIND_GA, KIND_Q, KIND_K, KIND_V, KIND_GB, KIND_MA, KIND_MB = range(8)
N_KINDS = 8
N_GROUP_KINDS = 6
HALVED_KINDS = (KIND_GA, KIND_GB, KIND_MA, KIND_MB)

LANES = 128
SUBLANES = 8
GROUP = 256
N_GROUPS = D_MODEL // GROUP
GROUP_COLS = N_GROUP_KINDS * GROUP
MERGE_COL0 = N_GROUPS * GROUP_COLS
TILE_T = 512
TILES_PER_STEP = 1
STEP_T = TILE_T * TILES_PER_STEP
N_CHUNKS = TILE_T // CHUNK
ROW_CHUNK = 32
STAGE_COLS = 256
STAGE_SLOTS = 4
PREP_ROWS = 64
HALO = SUBLANES
VMEM_LIMIT_BYTES = 56 * 1024 * 1024
TINY = 1e-37


def _dot(a, b):
    return jnp.dot(a, b, preferred_element_type=jnp.float32)


def _prepare_weights(win_hbm, wpa_hbm, wpb_hbm, wout_hbm, win_s, wpa_s, wpb_s, wout_s,
                     stage, sem):
    slabs = []
    for kind in range(N_KINDS):
        for c0 in range(0, D_MODEL, STAGE_COLS):
            src = win_hbm.at[:, pl.ds(kind * D_MODEL + c0, STAGE_COLS)]
            if kind < N_GROUP_KINDS:
                pieces = [(c - c0, (c // GROUP) * GROUP_COLS + kind * GROUP, GROUP)
                          for c in range(c0, c0 + STAGE_COLS, GROUP)]
            else:
                pieces = [(0, kind * D_MODEL + c0, STAGE_COLS)]
            slabs.append((src, win_s, pieces, kind in HALVED_KINDS))
    for src_hbm, dst, halve in ((wpa_hbm, wpa_s, True), (wpb_hbm, wpb_s, True),
                                (wout_hbm, wout_s, False)):
        for c0 in range(0, D_MODEL, STAGE_COLS):
            slabs.append((src_hbm.at[:, pl.ds(c0, STAGE_COLS)], dst,
                          [(0, c0, STAGE_COLS)], halve))

    def copy(n):
        slot = n % STAGE_SLOTS
        return pltpu.make_async_copy(slabs[n][0], stage.at[slot], sem.at[slot])

    for n in range(STAGE_SLOTS - 1):
        copy(n).start()
    for n, (_, dst, pieces, halve) in enumerate(slabs):
        if n + STAGE_SLOTS - 1 < len(slabs):
            copy(n + STAGE_SLOTS - 1).start()
        copy(n).wait()

        def convert(i, carry, slot=n % STAGE_SLOTS, dst=dst, pieces=pieces, halve=halve):
            rows = pl.ds(pl.multiple_of(i * PREP_ROWS, PREP_ROWS), PREP_ROWS)
            w = stage[slot, rows, :]
            if halve:
                w = 0.5 * w
            w = w.astype(jnp.bfloat16)
            for s0, d0, width in pieces:
                dst[rows, d0:d0 + width] = w[:, s0:s0 + width]
            return carry

        lax.fori_loop(0, D_MODEL // PREP_ROWS, convert, 0)


def _block_kernel(x_ref, nin_ref, win_hbm, cw_ref, cb_ref, gw_ref, gxb_ref, gab_ref, lam_ref,
                  gn_ref, wpa_hbm, wpb_hbm, wout_hbm, nfin_ref, cos_ref, sin_ref, idec_ref,
                  cdec_ref, sdec_ref, chdec_ref, out_ref,
                  win_s, wpa_s, wpb_s, wout_s, stage, sem,
                  h_ref, xa_ref, lru_ref, ya_ref, yb_ref, r_ref, hst_ref,
                  *, final_norm):
    @pl.when((pl.program_id(0) == 0) & (pl.program_id(1) == 0))
    def _():
        _prepare_weights(win_hbm, wpa_hbm, wpb_hbm, wout_hbm, win_s, wpa_s, wpb_s, wout_s,
                         stage, sem)

    def tile(i, row0):
        _tile_body(pl.program_id(1) * TILES_PER_STEP + i, row0,
                   x_ref, nin_ref, cw_ref, cb_ref, gw_ref, gxb_ref, gab_ref, lam_ref, gn_ref,
                   nfin_ref, cos_ref, sin_ref, idec_ref, cdec_ref, sdec_ref, chdec_ref, out_ref,
                   win_s, wpa_s, wpb_s, wout_s, h_ref, xa_ref, lru_ref, ya_ref, yb_ref, r_ref,
                   hst_ref, final_norm)

    for i in range(TILES_PER_STEP):
        tile(i, i * TILE_T)


def _tile_body(t, row0, x_ref, nin_ref, cw_ref, cb_ref, gw_ref, gxb_ref, gab_ref, lam_ref, gn_ref,
               nfin_ref, cos_ref, sin_ref, idec_ref, cdec_ref, sdec_ref, chdec_ref, out_ref,
               win_s, wpa_s, wpb_s, wout_s, h_ref, xa_ref, lru_ref, ya_ref, yb_ref, r_ref,
               hst_ref, final_norm):
    T = TILE_T
    bf16 = jnp.bfloat16

    @pl.when(t == 0)
    def _():
        r_ref[...] = jnp.zeros_like(r_ref)
        hst_ref[...] = jnp.zeros_like(hst_ref)
        xa_ref[:, 0:HALO, :] = jnp.zeros((D_MODEL // LANES, HALO, LANES), jnp.float32)

    g_in = nin_ref[...]
    for r0 in range(0, T, ROW_CHUNK):
        xr = x_ref[0, pl.ds(row0 + r0, ROW_CHUNK), :]
        ms = jnp.mean(xr * xr, axis=-1, keepdims=True)
        h_ref[r0:r0 + ROW_CHUNK, :] = (xr * lax.rsqrt(ms + EPS) * g_in).astype(bf16)

    n_rg = T // SUBLANES
    sub = lax.broadcasted_iota(jnp.int32, (n_rg, SUBLANES, GROUP), 1)
    half = RET_DK // 2
    both_halves = lambda m: jnp.concatenate([m, m], axis=1)

    def project(g):
        return _dot(h_ref[...], win_s[:, g * GROUP_COLS:(g + 1) * GROUP_COLS])

    proj = project(0)
    for g in range(N_GROUPS):
        cs = slice(g * GROUP, (g + 1) * GROUP)
        kind = lambda i, p=proj: p[:, i * GROUP:(i + 1) * GROUP]

        xa = kind(KIND_XA)
        xc_blocks = []
        for b in range(GROUP // LANES):
            blk = g * (GROUP // LANES) + b
            lc = slice(blk * LANES, (blk + 1) * LANES)
            xa_b = xa[:, b * LANES:(b + 1) * LANES]
            xa_ref[blk, HALO:HALO + T, :] = xa_b
            xc_b = cb_ref[:, lc] + cw_ref[CONV_WIDTH - 1:CONV_WIDTH, lc] * xa_b
            for k in range(CONV_WIDTH - 1):
                lo = HALO - (CONV_WIDTH - 1) + k
                xc_b = xc_b + cw_ref[k:k + 1, lc] * xa_ref[blk, lo:lo + T, :]
            xa_ref[blk, 0:HALO, :] = xa_ref[blk, T:T + HALO, :]
            xc_blocks.append(xc_b)
        xc = jnp.concatenate(xc_blocks, axis=1)
        gates = _dot(xc.astype(bf16), pltpu.bitcast(gw_ref[g], bf16))

        q, k, v, gb = kind(KIND_Q), kind(KIND_K), kind(KIND_V), kind(KIND_GB)
        ga = kind(KIND_GA)
        pos = pl.ds(pl.multiple_of(t * T, T), T)
        cos = cos_ref[pos, :]
        sin = sin_ref[pos, :]
        q1, q2 = q[:, :half], q[:, half:]
        k1, k2 = k[:, :half], k[:, half:]
        qr = jnp.concatenate([q1 * cos - q2 * sin, q1 * sin + q2 * cos], axis=1).astype(bf16)
        kr = jnp.concatenate([k1 * cos - k2 * sin, k1 * sin + k2 * cos], axis=1).astype(bf16)
        v_bf = v.astype(bf16)
        scores, kv = [], []
        for c in range(N_CHUNKS):
            rows = slice(c * CHUNK, (c + 1) * CHUNK)
            scores.append(lax.dot_general(qr[rows], kr[rows], (((1,), (1,)), ((), ())),
                                          preferred_element_type=jnp.float32))
            kv.append(lax.dot_general(kr[rows], (v[rows] * both_halves(sdec_ref[g])).astype(bf16),
                                      (((0,), (0,)), ((), ())),
                                      preferred_element_type=jnp.float32))

        if g + 1 < N_GROUPS:
            proj = project(g + 1)

        ti1 = 1.0 + jnp.tanh(gates[:, :GROUP] + 0.5 * gxb_ref[:, cs])
        tr1 = 1.0 + jnp.tanh(gates[:, GROUP:] + 0.5 * gab_ref[:, cs])
        nlam = -lam_ref[:, cs]
        softplus = jnp.maximum(nlam, 0.0) + jnp.log1p(jnp.exp(-jnp.abs(nlam)))
        nla = tr1 * ((0.5 * LRU_C) * softplus)
        a = jnp.exp(-nla)
        x4 = jnp.tanh(nla) * (a * (0.25 * a) + 0.25)
        mult = x4 * lax.rsqrt(jnp.maximum(x4, TINY))
        u = mult * (ti1 * xc)

        a3 = a.reshape(n_rg, SUBLANES, GROUP)
        u3 = u.reshape(n_rg, SUBLANES, GROUP)
        for s in (1, 2, 4):
            valid = sub >= s
            u_sh = jnp.where(valid, pltpu.roll(u3, s, 1), 0.0)
            a_sh = jnp.where(valid, pltpu.roll(a3, s, 1), 1.0)
            u3 = u3 + a3 * u_sh
            a3 = a3 * a_sh
        carry = hst_ref[:, cs]
        for j in range(n_rg):
            hj = u3[j] + a3[j] * carry
            lru_ref[j * SUBLANES:(j + 1) * SUBLANES, :] = hj
            carry = jnp.broadcast_to(hj[SUBLANES - 1:SUBLANES, :], (SUBLANES, GROUP))
        hst_ref[:, cs] = carry
        ya_ref[:, cs] = (ga * (1.0 + jnp.tanh(ga)) * lru_ref[...]).astype(bf16)

        silu_gb = gb * (1.0 + jnp.tanh(gb))
        gain = gn_ref[:, cs]
        r_state = r_ref[g]
        for c in range(N_CHUNKS):
            rows = slice(c * CHUNK, (c + 1) * CHUNK)
            cross = _dot(qr[rows], r_state.astype(bf16)) * both_halves(cdec_ref[g])
            r_state = chdec_ref[:, cs] * r_state + kv[c]
            inner = _dot((scores[c] * idec_ref[g]).astype(bf16), v_bf[rows])
            y = inner + cross
            mu = jnp.mean(y, axis=-1, keepdims=True)
            yc = y - mu
            var = jnp.mean(yc * yc, axis=-1, keepdims=True)
            gn = yc * lax.rsqrt(var + EPS) * gain
            yb_ref[rows, cs] = (silu_gb[rows] * gn).astype(bf16)
        r_ref[g] = r_state

    oa = _dot(ya_ref[...], wpa_s[...])
    ob = _dot(yb_ref[...], wpb_s[...])
    mg = _dot(h_ref[...], win_s[:, MERGE_COL0:MERGE_COL0 + 2 * D_MODEL])
    merged = ((1.0 + jnp.tanh(mg[:, :D_MODEL])) * oa
              + (1.0 + jnp.tanh(mg[:, D_MODEL:])) * ob).astype(bf16)
    out_ref[0, pl.ds(row0, T), :] = _dot(merged, wout_s[...])
    g_fin = nfin_ref[...]
    for r0 in range(0, T, ROW_CHUNK):
        rows = pl.ds(row0 + r0, ROW_CHUNK)
        xo = x_ref[0, rows, :] + out_ref[0, rows, :]
        if final_norm:
            ms = jnp.mean(xo * xo, axis=-1, keepdims=True)
            xo = xo * lax.rsqrt(ms + EPS) * g_fin
        out_ref[0, rows, :] = xo


def _block_diag_groups(w):
    per = GROUP // LRU_BW
    w4 = w.reshape(N_GROUPS, per, LRU_BW, LRU_BW)
    eye = jnp.eye(per, dtype=w.dtype)
    return jnp.einsum('gikn,ij->gikjn', w4, eye).reshape(N_GROUPS, GROUP, GROUP)


def _pack_rows(w):
    *lead, k, n = w.shape
    pairs = w.astype(jnp.bfloat16).reshape(*lead, k // 2, 2, n)
    return lax.bitcast_convert_type(jnp.swapaxes(pairs, -1, -2), jnp.uint32)


def _retention_tables(seq):
    f32 = jnp.float32
    half = RET_DK // 2
    freqs = ROPE_THETA ** (-jnp.arange(half, dtype=f32) / half)
    ang = jnp.arange(seq, dtype=f32)[:, None] * freqs[None, :]
    log_g = jnp.log1p(-(2.0 ** (-5.0 - jnp.arange(RET_HEADS, dtype=f32))))
    idx = jnp.arange(CHUNK, dtype=f32)
    diff = idx[:, None] - idx[None, :]
    inner = jnp.where(diff >= 0, jnp.exp(jnp.maximum(diff, 0.0)[None] * log_g[:, None, None]), 0.0)
    cross = jnp.exp((idx[:, None] + 1.0) * log_g[None, :])
    state = jnp.exp((CHUNK - 1.0 - idx[:, None]) * log_g[None, :])
    chunk = jnp.exp(CHUNK * log_g)[None, :]
    per_head = lambda m: jnp.broadcast_to(m.T[:, :, None], (RET_HEADS, CHUNK, LANES))
    k_scale = RET_DK ** -0.5
    assert k_scale == 2.0 ** -4
    return (jnp.cos(ang), jnp.sin(ang), inner * k_scale, per_head(cross),
            per_head(state * k_scale), jnp.repeat(chunk, RET_DV, axis=1))


def _const_spec(shape):
    zeros = (0,) * len(shape)
    return pl.BlockSpec(shape, lambda b, t: zeros, pipeline_mode=pl.Buffered(1))


def _layer(x, nin, win, cw, cb, gxw, gxb, gaw, gab, lam, gn, wpa, wpb, wout, nfin, tables,
           final_norm):
    B, S, D = x.shape
    assert D == D_MODEL and S % STEP_T == 0
    assert win.shape == (D, N_KINDS * D) and wpa.shape == wpb.shape == wout.shape == (D, D)
    bf16 = jnp.bfloat16
    row = lambda v: v.reshape(1, -1)
    gw = _pack_rows(0.5 * jnp.concatenate([_block_diag_groups(gxw), _block_diag_groups(gaw)], axis=-1))
    cos, sin, idec, cdec, sdec, chdec = tables
    operands = (
        x, row(nin), win, cw, row(cb), gw, row(gxb), row(gab), row(lam), row(gn),
        wpa, wpb, wout, row(nfin), cos, sin, idec, cdec, sdec, chdec)
    hbm_operands = (2, 10, 11, 12)
    tile_spec = pl.BlockSpec((1, STEP_T, D), lambda b, t: (b, t, 0))
    in_specs = []
    for i, o in enumerate(operands):
        if i == 0:
            in_specs.append(tile_spec)
        elif i in hbm_operands:
            in_specs.append(pl.BlockSpec(memory_space=pl.ANY))
        else:
            in_specs.append(_const_spec(o.shape))
    scratch = [
        pltpu.VMEM((D, N_KINDS * D), bf16),
        pltpu.VMEM((D, D), bf16),
        pltpu.VMEM((D, D), bf16),
        pltpu.VMEM((D, D), bf16),
        pltpu.VMEM((STAGE_SLOTS, D, STAGE_COLS), jnp.float32),
        pltpu.SemaphoreType.DMA((STAGE_SLOTS,)),
        pltpu.VMEM((TILE_T, D), bf16),
        pltpu.VMEM((D // LANES, TILE_T + HALO, LANES), jnp.float32),
        pltpu.VMEM((TILE_T, GROUP), jnp.float32),
        pltpu.VMEM((TILE_T, D), bf16),
        pltpu.VMEM((TILE_T, D), bf16),
        pltpu.VMEM((RET_HEADS, RET_DK, RET_DV), jnp.float32),
        pltpu.VMEM((SUBLANES, D), jnp.float32),
    ]
    return pl.pallas_call(
        functools.partial(_block_kernel, final_norm=final_norm),
        grid=(B, S // STEP_T),
        in_specs=in_specs,
        out_specs=tile_spec,
        out_shape=jax.ShapeDtypeStruct(x.shape, x.dtype),
        scratch_shapes=scratch,
        compiler_params=pltpu.CompilerParams(
            dimension_semantics=("arbitrary", "arbitrary"),
            vmem_limit_bytes=VMEM_LIMIT_BYTES),
        name="hybrid_block",
    )(*operands)


def kernel(x, norm_in, w_in, conv_w, conv_b, gate_x_w, gate_x_b, gate_a_w, gate_a_b, lru_lambda, gn_gain, w_proj_a, w_proj_b, w_out, norm_final):
    depth = w_in.shape[0]
    tables = _retention_tables(x.shape[1])
    for l in range(depth):
        x = _layer(x, norm_in[l], w_in[l], conv_w[l], conv_b[l], gate_x_w[l], gate_x_b[l],
                   gate_a_w[l], gate_a_b[l], lru_lambda[l], gn_gain[l].reshape(-1),
                   w_proj_a[l], w_proj_b[l], w_out[l], norm_final, tables,
                   final_norm=(l == depth - 1))
    return x
```

```python
import functools

import jax
import jax.numpy as jnp
from jax import lax
from jax.experimental import pallas as pl
from jax.experimental.pallas import tpu as pltpu

D_MODEL = 1024
LRU_BLOCKS = 16
LRU_BW = D_MODEL // LRU_BLOCKS
CONV_WIDTH = 4
LRU_C = 8.0
RET_HEADS = 4
RET_DK = 256
RET_DV = 256
CHUNK = 256
ROPE_THETA = 10000.0
EPS = 1e-6

KIND_XA, KIND_GA, KIND_Q, KIND_K, KIND_V, KIND_GB, KIND_MA, KIND_MB = range(8)
N_KINDS = 8
N_GROUP_KINDS = 6
HALVED_KINDS = (KIND_GA, KIND_GB, KIND_MA, KIND_MB)

LANES = 128
SUBLANES = 8
GROUP = 256
N_GROUPS = D_MODEL // GROUP
GROUP_COLS = N_GROUP_KINDS * GROUP
MERGE_COL0 = N_GROUPS * GROUP_COLS
TILE_T = 512
TILES_PER_STEP = 1
STEP_T = TILE_T * TILES_PER_STEP
N_CHUNKS = TILE_T // CHUNK
ROW_CHUNK = 32
STAGE_COLS = 256
STAGE_SLOTS = 4
PREP_ROWS = 64
HALO = SUBLANES
VMEM_LIMIT_BYTES = 56 * 1024 * 1024
TINY = 1e-37


def _dot(a, b):
    return jnp.dot(a, b, preferred_element_type=jnp.float32)


def _prepare_weights(win_hbm, wpa_hbm, wpb_hbm, wout_hbm, win_s, wpa_s, wpb_s, wout_s,
                     stage, sem):
    slabs = []
    for kind in range(N_KINDS):
        for c0 in range(0, D_MODEL, STAGE_COLS):
            src = win_hbm.at[:, pl.ds(kind * D_MODEL + c0, STAGE_COLS)]
            if kind < N_GROUP_KINDS:
                pieces = [(c - c0, (c // GROUP) * GROUP_COLS + kind * GROUP, GROUP)
                          for c in range(c0, c0 + STAGE_COLS, GROUP)]
            else:
                pieces = [(0, kind * D_MODEL + c0, STAGE_COLS)]
            slabs.append((src, win_s, pieces, kind in HALVED_KINDS))
    for src_hbm, dst, halve in ((wpa_hbm, wpa_s, True), (wpb_hbm, wpb_s, True),
                                (wout_hbm, wout_s, False)):
        for c0 in range(0, D_MODEL, STAGE_COLS):
            slabs.append((src_hbm.at[:, pl.ds(c0, STAGE_COLS)], dst,
                          [(0, c0, STAGE_COLS)], halve))

    def copy(n):
        slot = n % STAGE_SLOTS
        return pltpu.make_async_copy(slabs[n][0], stage.at[slot], sem.at[slot])

    for n in range(STAGE_SLOTS - 1):
        copy(n).start()
    for n, (_, dst, pieces, halve) in enumerate(slabs):
        if n + STAGE_SLOTS - 1 < len(slabs):
            copy(n + STAGE_SLOTS - 1).start()
        copy(n).wait()

        def convert(i, carry, slot=n % STAGE_SLOTS, dst=dst, pieces=pieces, halve=halve):
            rows = pl.ds(pl.multiple_of(i * PREP_ROWS, PREP_ROWS), PREP_ROWS)
            w = stage[slot, rows, :]
            if halve:
                w = 0.5 * w
            w = w.astype(jnp.bfloat16)
            for s0, d0, width in pieces:
                dst[rows, d0:d0 + width] = w[:, s0:s0 + width]
            return carry

        lax.fori_loop(0, D_MODEL // PREP_ROWS, convert, 0)


def _block_kernel(x_ref, nin_ref, win_hbm, cw_ref, cb_ref, gw_ref, gxb_ref, gab_ref, lam_ref,
                  gn_ref, wpa_hbm, wpb_hbm, wout_hbm, nfin_ref, cos_ref, sin_ref, idec_ref,
                  cdec_ref, sdec_ref, chdec_ref, out_ref,
                  win_s, wpa_s, wpb_s, wout_s, stage, sem,
                  h_ref, xa_ref, lru_ref, ya_ref, yb_ref, r_ref, hst_ref,
                  *, final_norm):
    @pl.when((pl.program_id(0) == 0) & (pl.program_id(1) == 0))
    def _():
        _prepare_weights(win_hbm, wpa_hbm, wpb_hbm, wout_hbm, win_s, wpa_s, wpb_s, wout_s,
                         stage, sem)

    def tile(i, row0):
        _tile_body(pl.program_id(1) * TILES_PER_STEP + i, row0,
                   x_ref, nin_ref, cw_ref, cb_ref, gw_ref, gxb_ref, gab_ref, lam_ref, gn_ref,
                   nfin_ref, cos_ref, sin_ref, idec_ref, cdec_ref, sdec_ref, chdec_ref, out_ref,
                   win_s, wpa_s, wpb_s, wout_s, h_ref, xa_ref, lru_ref, ya_ref, yb_ref, r_ref,
                   hst_ref, final_norm)

    for i in range(TILES_PER_STEP):
        tile(i, i * TILE_T)


def _tile_body(t, row0, x_ref, nin_ref, cw_ref, cb_ref, gw_ref, gxb_ref, gab_ref, lam_ref, gn_ref,
               nfin_ref, cos_ref, sin_ref, idec_ref, cdec_ref, sdec_ref, chdec_ref, out_ref,
               win_s, wpa_s, wpb_s, wout_s, h_ref, xa_ref, lru_ref, ya_ref, yb_ref, r_ref,
               hst_ref, final_norm):
    T = TILE_T
    bf16 = jnp.bfloat16

    @pl.when(t == 0)
    def _():
        r_ref[...] = jnp.zeros_like(r_ref)
        hst_ref[...] = jnp.zeros_like(hst_ref)
        xa_ref[:, 0:HALO, :] = jnp.zeros((D_MODEL // LANES, HALO, LANES), jnp.float32)

    g_in = nin_ref[...]
    for r0 in range(0, T, ROW_CHUNK):
        xr = x_ref[0, pl.ds(row0 + r0, ROW_CHUNK), :]
        ms = jnp.mean(xr * xr, axis=-1, keepdims=True)
        h_ref[r0:r0 + ROW_CHUNK, :] = (xr * lax.rsqrt(ms + EPS) * g_in).astype(bf16)

    n_rg = T // SUBLANES
    sub = lax.broadcasted_iota(jnp.int32, (n_rg, SUBLANES, GROUP), 1)
    half = RET_DK // 2
    both_halves = lambda m: jnp.concatenate([m, m], axis=1)

    def project(g):
        return _dot(h_ref[...], win_s[:, g * GROUP_COLS:(g + 1) * GROUP_COLS])

    proj = project(0)
    for g in range(N_GROUPS):
        cs = slice(g * GROUP, (g + 1) * GROUP)
        kind = lambda i, p=proj: p[:, i * GROUP:(i + 1) * GROUP]

        xa = kind(KIND_XA)
        xc_blocks = []
        for b in range(GROUP // LANES):
            blk = g * (GROUP // LANES) + b
            lc = slice(blk * LANES, (blk + 1) * LANES)
            xa_b = xa[:, b * LANES:(b + 1) * LANES]
            xa_ref[blk, HALO:HALO + T, :] = xa_b
            xc_b = cb_ref[:, lc] + cw_ref[CONV_WIDTH - 1:CONV_WIDTH, lc] * xa_b
            for k in range(CONV_WIDTH - 1):
                lo = HALO - (CONV_WIDTH - 1) + k
                xc_b = xc_b + cw_ref[k:k + 1, lc] * xa_ref[blk, lo:lo + T, :]
            xa_ref[blk, 0:HALO, :] = xa_ref[blk, T:T + HALO, :]
            xc_blocks.append(xc_b)
        xc = jnp.concatenate(xc_blocks, axis=1)
        gates = _dot(xc.astype(bf16), pltpu.bitcast(gw_ref[g], bf16))

        q, k, v, gb = kind(KIND_Q), kind(KIND_K), kind(KIND_V), kind(KIND_GB)
        ga = kind(KIND_GA)
        pos = pl.ds(pl.multiple_of(t * T, T), T)
        cos = cos_ref[pos, :]
        sin = sin_ref[pos, :]
        q1, q2 = q[:, :half], q[:, half:]
        k1, k2 = k[:, :half], k[:, half:]
        qr = jnp.concatenate([q1 * cos - q2 * sin, q1 * sin + q2 * cos], axis=1).astype(bf16)
        kr = jnp.concatenate([k1 * cos - k2 * sin, k1 * sin + k2 * cos], axis=1).astype(bf16)
        v_bf = v.astype(bf16)
        scores, kv = [], []
        for c in range(N_CHUNKS):
            rows = slice(c * CHUNK, (c + 1) * CHUNK)
            scores.append(lax.dot_general(qr[rows], kr[rows], (((1,), (1,)), ((), ())),
                                          preferred_element_type=jnp.float32))
            kv.append(lax.dot_general(kr[rows], (v[rows] * both_halves(sdec_ref[g])).astype(bf16),
                                      (((0,), (0,)), ((), ())),
                                      preferred_element_type=jnp.float32))

        if g + 1 < N_GROUPS:
            proj = project(g + 1)
        else:
            mg = _dot(h_ref[...], win_s[:, MERGE_COL0:MERGE_COL0 + 2 * D_MODEL])

        ti1 = 1.0 + jnp.tanh(gates[:, :GROUP] + 0.5 * gxb_ref[:, cs])
        tr1 = 1.0 + jnp.tanh(gates[:, GROUP:] + 0.5 * gab_ref[:, cs])
        nlam = -lam_ref[:, cs]
        softplus = jnp.maximum(nlam, 0.0) + jnp.log1p(jnp.exp(-jnp.abs(nlam)))
        nla = tr1 * ((0.5 * LRU_C) * softplus)
        a = jnp.exp(-nla)
        x4 = jnp.tanh(nla) * (a * (0.25 * a) + 0.25)
        mult = x4 * lax.rsqrt(jnp.maximum(x4, TINY))
        u = mult * (ti1 * xc)

        a3 = a.reshape(n_rg, SUBLANES, GROUP)
        u3 = u.reshape(n_rg, SUBLANES, GROUP)
        for s in (1, 2, 4):
            valid = sub >= s
            u_sh = jnp.where(valid, pltpu.roll(u3, s, 1), 0.0)
            a_sh = jnp.where(valid, pltpu.roll(a3, s, 1), 1.0)
            u3 = u3 + a3 * u_sh
            a3 = a3 * a_sh
        carry = hst_ref[:, cs]
        for j in range(n_rg):
            hj = u3[j] + a3[j] * carry
            lru_ref[j * SUBLANES:(j + 1) * SUBLANES, :] = hj
            carry = jnp.broadcast_to(hj[SUBLANES - 1:SUBLANES, :], (SUBLANES, GROUP))
        hst_ref[:, cs] = carry
        ya_ref[:, cs] = (ga * (1.0 + jnp.tanh(ga)) * lru_ref[...]).astype(bf16)

        silu_gb = gb * (1.0 + jnp.tanh(gb))
        gain = gn_ref[:, cs]
        r_state = r_ref[g]
        for c in range(N_CHUNKS):
            rows = slice(c * CHUNK, (c + 1) * CHUNK)
            cross = _dot(qr[rows], r_state.astype(bf16)) * both_halves(cdec_ref[g])
            r_state = chdec_ref[:, cs] * r_state + kv[c]
            inner = _dot((scores[c] * idec_ref[g]).astype(bf16), v_bf[rows])
            y = inner + cross
            mu = jnp.mean(y, axis=-1, keepdims=True)
            yc = y - mu
            var = jnp.mean(yc * yc, axis=-1, keepdims=True)
            gn = yc * lax.rsqrt(var + EPS) * gain
            yb_ref[rows, cs] = (silu_gb[rows] * gn).astype(bf16)
        r_ref[g] = r_state

    oa = _dot(ya_ref[...], wpa_s[...])
    ob = _dot(yb_ref[...], wpb_s[...])
    merged = ((1.0 + jnp.tanh(mg[:, :D_MODEL])) * oa
              + (1.0 + jnp.tanh(mg[:, D_MODEL:])) * ob).astype(bf16)
    out_ref[0, pl.ds(row0, T), :] = _dot(merged, wout_s[...])
    g_fin = nfin_ref[...]
    for r0 in range(0, T, ROW_CHUNK):
        rows = pl.ds(row0 + r0, ROW_CHUNK)
        xo = x_ref[0, rows, :] + out_ref[0, rows, :]
        if final_norm:
            ms = jnp.mean(xo * xo, axis=-1, keepdims=True)
            xo = xo * lax.rsqrt(ms + EPS) * g_fin
        out_ref[0, rows, :] = xo


def _block_diag_groups(w):
    per = GROUP // LRU_BW
    w4 = w.reshape(N_GROUPS, per, LRU_BW, LRU_BW)
    eye = jnp.eye(per, dtype=w.dtype)
    return jnp.einsum('gikn,ij->gikjn', w4, eye).reshape(N_GROUPS, GROUP, GROUP)


def _pack_rows(w):
    *lead, k, n = w.shape
    pairs = w.astype(jnp.bfloat16).reshape(*lead, k // 2, 2, n)
    return lax.bitcast_convert_type(jnp.swapaxes(pairs, -1, -2), jnp.uint32)


def _retention_tables(seq):
    f32 = jnp.float32
    half = RET_DK // 2
    freqs = ROPE_THETA ** (-jnp.arange(half, dtype=f32) / half)
    ang = jnp.arange(seq, dtype=f32)[:, None] * freqs[None, :]
    log_g = jnp.log1p(-(2.0 ** (-5.0 - jnp.arange(RET_HEADS, dtype=f32))))
    idx = jnp.arange(CHUNK, dtype=f32)
    diff = idx[:, None] - idx[None, :]
    inner = jnp.where(diff >= 0, jnp.exp(jnp.maximum(diff, 0.0)[None] * log_g[:, None, None]), 0.0)
    cross = jnp.exp((idx[:, None] + 1.0) * log_g[None, :])
    state = jnp.exp((CHUNK - 1.0 - idx[:, None]) * log_g[None, :])
    chunk = jnp.exp(CHUNK * log_g)[None, :]
    per_head = lambda m: jnp.broadcast_to(m.T[:, :, None], (RET_HEADS, CHUNK, LANES))
    k_scale = RET_DK ** -0.5
    assert k_scale == 2.0 ** -4
    return (jnp.cos(ang), jnp.sin(ang), inner * k_scale, per_head(cross),
            per_head(state * k_scale), jnp.repeat(chunk, RET_DV, axis=1))


def _const_spec(shape):
    zeros = (0,) * len(shape)
    return pl.BlockSpec(shape, lambda b, t: zeros, pipeline_mode=pl.Buffered(1))


def _layer(x, nin, win, cw, cb, gxw, gxb, gaw, gab, lam, gn, wpa, wpb, wout, nfin, tables,
           final_norm):
    B, S, D = x.shape
    assert D == D_MODEL and S % STEP_T == 0
    assert win.shape == (D, N_KINDS * D) and wpa.shape == wpb.shape == wout.shape == (D, D)
    bf16 = jnp.bfloat16
    row = lambda v: v.reshape(1, -1)
    gw = _pack_rows(0.5 * jnp.concatenate([_block_diag_groups(gxw), _block_diag_groups(gaw)], axis=-1))
    cos, sin, idec, cdec, sdec, chdec = tables
    operands = (
        x, row(nin), win, cw, row(cb), gw, row(gxb), row(gab), row(lam), row(gn),
        wpa, wpb, wout, row(nfin), cos, sin, idec, cdec, sdec, chdec)
    hbm_operands = (2, 10, 11, 12)
    tile_spec = pl.BlockSpec((1, STEP_T, D), lambda b, t: (b, t, 0))
    in_specs = []
    for i, o in enumerate(operands):
        if i == 0:
            in_specs.append(tile_spec)
        elif i in hbm_operands:
            in_specs.append(pl.BlockSpec(memory_space=pl.ANY))
        else:
            in_specs.append(_const_spec(o.shape))
    scratch = [
        pltpu.VMEM((D, N_KINDS * D), bf16),
        pltpu.VMEM((D, D), bf16),
        pltpu.VMEM((D, D), bf16),
        pltpu.VMEM((D, D), bf16),
        pltpu.VMEM((STAGE_SLOTS, D, STAGE_COLS), jnp.float32),
        pltpu.SemaphoreType.DMA((STAGE_SLOTS,)),
        pltpu.VMEM((TILE_T, D), bf16),
        pltpu.VMEM((D // LANES, TILE_T + HALO, LANES), jnp.float32),
        pltpu.VMEM((TILE_T, GROUP), jnp.float32),
        pltpu.VMEM((TILE_T, D), bf16),
        pltpu.VMEM((TILE_T, D), bf16),
        pltpu.VMEM((RET_HEADS, RET_DK, RET_DV), jnp.float32),
        pltpu.VMEM((SUBLANES, D), jnp.float32),
    ]
    return pl.pallas_call(
        functools.partial(_block_kernel, final_norm=final_norm),
        grid=(B, S // STEP_T),
        in_specs=in_specs,
        out_specs=tile_spec,
        out_shape=jax.ShapeDtypeStruct(x.shape, x.dtype),
        scratch_shapes=scratch,
        compiler_params=pltpu.CompilerParams(
            dimension_semantics=("arbitrary", "arbitrary"),
            vmem_limit_bytes=VMEM_LIMIT_BYTES),
        name="hybrid_block",
    )(*operands)


def kernel(x, norm_in, w_in, conv_w, conv_b, gate_x_w, gate_x_b, gate_a_w, gate_a_b, lru_lambda, gn_gain, w_proj_a, w_proj_b, w_out, norm_final):
    depth = w_in.shape[0]
    tables = _retention_tables(x.shape[1])
    for l in range(depth):
        x = _layer(x, norm_in[l], w_in[l], conv_w[l], conv_b[l], gate_x_w[l], gate_x_b[l],
                   gate_a_w[l], gate_a_b[l], lru_lambda[l], gn_gain[l].reshape(-1),
                   w_proj_a[l], w_proj_b[l], w_out[l], norm_final, tables,
                   final_norm=(l == depth - 1))
    return x
```

```python
import functools

import jax
import jax.numpy as jnp
from jax import lax
from jax.experimental import pallas as pl
from jax.experimental.pallas import tpu as pltpu

D_MODEL = 1024
LRU_BLOCKS = 16
LRU_BW = D_MODEL // LRU_BLOCKS
CONV_WIDTH = 4
LRU_C = 8.0
RET_HEADS = 4
RET_DK = 256
RET_DV = 256
CHUNK = 256
ROPE_THETA = 10000.0
EPS = 1e-6

KIND_XA, KIND_GA, KIND_Q, KIND_K, KIND_V, KIND_GB, KIND_MA, KIND_MB = range(8)
N_KINDS = 8
N_GROUP_KINDS = 6
HALVED_KINDS = (KIND_GA, KIND_GB, KIND_MA, KIND_MB)

LANES = 128
SUBLANES = 8
GROUP = 256
N_GROUPS = D_MODEL // GROUP
GROUP_COLS = N_GROUP_KINDS * GROUP
MERGE_COL0 = N_GROUPS * GROUP_COLS
TILE_T = 512
TILES_PER_STEP = 1
STEP_T = TILE_T * TILES_PER_STEP
N_CHUNKS = TILE_T // CHUNK
ROW_CHUNK = 32
OUT_ROWS = 128
FIRST_PART_KINDS = 3
STAGE_COLS = 256
STAGE_SLOTS = 4
PREP_ROWS = 64
HALO = SUBLANES
VMEM_LIMIT_BYTES = 56 * 1024 * 1024
TINY = 1e-37


def _dot(a, b):
    return jnp.dot(a, b, preferred_element_type=jnp.float32)


def _prepare_weights(win_hbm, wpa_hbm, wpb_hbm, wout_hbm, win_s, wpa_s, wpb_s, wout_s,
                     stage, sem):
    slabs = []
    for kind in range(N_KINDS):
        for c0 in range(0, D_MODEL, STAGE_COLS):
            src = win_hbm.at[:, pl.ds(kind * D_MODEL + c0, STAGE_COLS)]
            if kind < N_GROUP_KINDS:
                pieces = [(c - c0, (c // GROUP) * GROUP_COLS + kind * GROUP, GROUP)
                          for c in range(c0, c0 + STAGE_COLS, GROUP)]
            else:
                pieces = [(0, kind * D_MODEL + c0, STAGE_COLS)]
            slabs.append((src, win_s, pieces, kind in HALVED_KINDS))
    for src_hbm, dst, halve in ((wpa_hbm, wpa_s, True), (wpb_hbm, wpb_s, True),
                                (wout_hbm, wout_s, False)):
        for c0 in range(0, D_MODEL, STAGE_COLS):
            slabs.append((src_hbm.at[:, pl.ds(c0, STAGE_COLS)], dst,
                          [(0, c0, STAGE_COLS)], halve))

    def copy(n):
        slot = n % STAGE_SLOTS
        return pltpu.make_async_copy(slabs[n][0], stage.at[slot], sem.at[slot])

    for n in range(STAGE_SLOTS - 1):
        copy(n).start()
    for n, (_, dst, pieces, halve) in enumerate(slabs):
        if n + STAGE_SLOTS - 1 < len(slabs):
            copy(n + STAGE_SLOTS - 1).start()
        copy(n).wait()

        def convert(i, carry, slot=n % STAGE_SLOTS, dst=dst, pieces=pieces, halve=halve):
            rows = pl.ds(pl.multiple_of(i * PREP_ROWS, PREP_ROWS), PREP_ROWS)
            w = stage[slot, rows, :]
            if halve:
                w = 0.5 * w
            w = w.astype(jnp.bfloat16)
            for s0, d0, width in pieces:
                dst[rows, d0:d0 + width] = w[:, s0:s0 + width]
            return carry

        lax.fori_loop(0, D_MODEL // PREP_ROWS, convert, 0)


def _block_kernel(x_ref, nin_ref, win_hbm, cw_ref, cb_ref, gw_ref, gxb_ref, gab_ref, lam_ref,
                  gn_ref, wpa_hbm, wpb_hbm, wout_hbm, nfin_ref, cos_ref, sin_ref, idec_ref,
                  cdec_ref, sdec_ref, chdec_ref, out_ref,
                  win_s, wpa_s, wpb_s, wout_s, stage, sem,
                  h_ref, xa_ref, lru_ref, ya_ref, yb_ref, r_ref, hst_ref,
                  *, final_norm):
    @pl.when((pl.program_id(0) == 0) & (pl.program_id(1) == 0))
    def _():
        _prepare_weights(win_hbm, wpa_hbm, wpb_hbm, wout_hbm, win_s, wpa_s, wpb_s, wout_s,
                         stage, sem)

    def tile(i, row0):
        _tile_body(pl.program_id(1) * TILES_PER_STEP + i, row0,
                   x_ref, nin_ref, cw_ref, cb_ref, gw_ref, gxb_ref, gab_ref, lam_ref, gn_ref,
                   nfin_ref, cos_ref, sin_ref, idec_ref, cdec_ref, sdec_ref, chdec_ref, out_ref,
                   win_s, wpa_s, wpb_s, wout_s, h_ref, xa_ref, lru_ref, ya_ref, yb_ref, r_ref,
                   hst_ref, final_norm)

    for i in range(TILES_PER_STEP):
        tile(i, i * TILE_T)


def _tile_body(t, row0, x_ref, nin_ref, cw_ref, cb_ref, gw_ref, gxb_ref, gab_ref, lam_ref, gn_ref,
               nfin_ref, cos_ref, sin_ref, idec_ref, cdec_ref, sdec_ref, chdec_ref, out_ref,
               win_s, wpa_s, wpb_s, wout_s, h_ref, xa_ref, lru_ref, ya_ref, yb_ref, r_ref,
               hst_ref, final_norm):
    T = TILE_T
    bf16 = jnp.bfloat16

    @pl.when(t == 0)
    def _():
        r_ref[...] = jnp.zeros_like(r_ref)
        hst_ref[...] = jnp.zeros_like(hst_ref)
        xa_ref[:, 0:HALO, :] = jnp.zeros((D_MODEL // LANES, HALO, LANES), jnp.float32)

    g_in = nin_ref[...]
    for r0 in range(0, T, ROW_CHUNK):
        xr = x_ref[0, pl.ds(row0 + r0, ROW_CHUNK), :]
        ms = jnp.mean(xr * xr, axis=-1, keepdims=True)
        h_ref[r0:r0 + ROW_CHUNK, :] = (xr * lax.rsqrt(ms + EPS) * g_in).astype(bf16)

    n_rg = T // SUBLANES
    sub = lax.broadcasted_iota(jnp.int32, (n_rg, SUBLANES, GROUP), 1)
    half = RET_DK // 2
    both_halves = lambda m: jnp.concatenate([m, m], axis=1)

    part_kinds = ((0, FIRST_PART_KINDS), (FIRST_PART_KINDS, N_GROUP_KINDS))

    def project(g, part):
        lo, hi = part_kinds[part]
        return _dot(h_ref[...], win_s[:, g * GROUP_COLS + lo * GROUP:g * GROUP_COLS + hi * GROUP])

    def merge_gate(part):
        c0 = MERGE_COL0 + part * D_MODEL
        return _dot(h_ref[...], win_s[:, c0:c0 + D_MODEL])

    parts = [project(0, 0), project(0, 1)]
    for g in range(N_GROUPS):
        cs = slice(g * GROUP, (g + 1) * GROUP)
        def kind(i, p=tuple(parts)):
            part = int(i >= FIRST_PART_KINDS)
            j = i - part_kinds[part][0]
            return p[part][:, j * GROUP:(j + 1) * GROUP]
        if g + 1 < N_GROUPS:
            parts[0] = project(g + 1, 0)
        else:
            mg_a = merge_gate(0)

        xa = kind(KIND_XA)
        xc_blocks = []
        for b in range(GROUP // LANES):
            blk = g * (GROUP // LANES) + b
            lc = slice(blk * LANES, (blk + 1) * LANES)
            xa_b = xa[:, b * LANES:(b + 1) * LANES]
            xa_ref[blk, HALO:HALO + T, :] = xa_b
            xc_b = cb_ref[:, lc] + cw_ref[CONV_WIDTH - 1:CONV_WIDTH, lc] * xa_b
            for k in range(CONV_WIDTH - 1):
                lo = HALO - (CONV_WIDTH - 1) + k
                xc_b = xc_b + cw_ref[k:k + 1, lc] * xa_ref[blk, lo:lo + T, :]
            xa_ref[blk, 0:HALO, :] = xa_ref[blk, T:T + HALO, :]
            xc_blocks.append(xc_b)
        xc = jnp.concatenate(xc_blocks, axis=1)
        gates = _dot(xc.astype(bf16), pltpu.bitcast(gw_ref[g], bf16))

        q, k, v, gb = kind(KIND_Q), kind(KIND_K), kind(KIND_V), kind(KIND_GB)
        ga = kind(KIND_GA)
        pos = pl.ds(pl.multiple_of(t * T, T), T)
        cos = cos_ref[pos, :]
        sin = sin_ref[pos, :]
        q1, q2 = q[:, :half], q[:, half:]
        k1, k2 = k[:, :half], k[:, half:]
        qr = jnp.concatenate([q1 * cos - q2 * sin, q1 * sin + q2 * cos], axis=1).astype(bf16)
        kr = jnp.concatenate([k1 * cos - k2 * sin, k1 * sin + k2 * cos], axis=1).astype(bf16)
        v_bf = v.astype(bf16)
        scores, kv = [], []
        for c in range(N_CHUNKS):
            rows = slice(c * CHUNK, (c + 1) * CHUNK)
            scores.append(lax.dot_general(qr[rows], kr[rows], (((1,), (1,)), ((), ())),
                                          preferred_element_type=jnp.float32))
            kv.append(lax.dot_general(kr[rows], (v[rows] * both_halves(sdec_ref[g])).astype(bf16),
                                      (((0,), (0,)), ((), ())),
                                      preferred_element_type=jnp.float32))

        if g + 1 < N_GROUPS:
            parts[1] = project(g + 1, 1)
        else:
            mg_b = merge_gate(1)

        ti1 = 1.0 + jnp.tanh(gates[:, :GROUP] + 0.5 * gxb_ref[:, cs])
        tr1 = 1.0 + jnp.tanh(gates[:, GROUP:] + 0.5 * gab_ref[:, cs])
        nlam = -lam_ref[:, cs]
        softplus = jnp.maximum(nlam, 0.0) + jnp.log1p(jnp.exp(-jnp.abs(nlam)))
        nla = tr1 * ((0.5 * LRU_C) * softplus)
        a = jnp.exp(-nla)
        x4 = jnp.tanh(nla) * (a * (0.25 * a) + 0.25)
        mult = x4 * lax.rsqrt(jnp.maximum(x4, TINY))
        u = mult * (ti1 * xc)

        a3 = a.reshape(n_rg, SUBLANES, GROUP)
        u3 = u.reshape(n_rg, SUBLANES, GROUP)
        for s in (1, 2, 4):
            valid = sub >= s
            u_sh = jnp.where(valid, pltpu.roll(u3, s, 1), 0.0)
            a_sh = jnp.where(valid, pltpu.roll(a3, s, 1), 1.0)
            u3 = u3 + a3 * u_sh
            a3 = a3 * a_sh
        carry = hst_ref[:, cs]
        for j in range(n_rg):
            hj = u3[j] + a3[j] * carry
            lru_ref[j * SUBLANES:(j + 1) * SUBLANES, :] = hj
            carry = jnp.broadcast_to(hj[SUBLANES - 1:SUBLANES, :], (SUBLANES, GROUP))
        hst_ref[:, cs] = carry
        ya_ref[:, cs] = (ga * (1.0 + jnp.tanh(ga)) * lru_ref[...]).astype(bf16)

        silu_gb = gb * (1.0 + jnp.tanh(gb))
        gain = gn_ref[:, cs]
        r_state = r_ref[g]
        for c in range(N_CHUNKS):
            rows = slice(c * CHUNK, (c + 1) * CHUNK)
            cross = _dot(qr[rows], r_state.astype(bf16)) * both_halves(cdec_ref[g])
            r_state = chdec_ref[:, cs] * r_state + kv[c]
            inner = _dot((scores[c] * idec_ref[g]).astype(bf16), v_bf[rows])
            y = inner + cross
            mu = jnp.mean(y, axis=-1, keepdims=True)
            yc = y - mu
            var = jnp.mean(yc * yc, axis=-1, keepdims=True)
            gn = yc * lax.rsqrt(var + EPS) * gain
            yb_ref[rows, cs] = (silu_gb[rows] * gn).astype(bf16)
        r_ref[g] = r_state

    oa = _dot(ya_ref[...], wpa_s[...])
    ob = _dot(yb_ref[...], wpb_s[...])
    merged = ((1.0 + jnp.tanh(mg_a)) * oa + (1.0 + jnp.tanh(mg_b)) * ob).astype(bf16)
    g_fin = nfin_ref[...]
    for m0 in range(0, T, OUT_ROWS):
        out_ref[0, pl.ds(row0 + m0, OUT_ROWS), :] = _dot(merged[m0:m0 + OUT_ROWS], wout_s[...])
        for r0 in range(m0, m0 + OUT_ROWS, ROW_CHUNK):
            rows = pl.ds(row0 + r0, ROW_CHUNK)
            xo = x_ref[0, rows, :] + out_ref[0, rows, :]
            if final_norm:
                ms = jnp.mean(xo * xo, axis=-1, keepdims=True)
                xo = xo * lax.rsqrt(ms + EPS) * g_fin
            out_ref[0, rows, :] = xo


def _block_diag_groups(w):
    per = GROUP // LRU_BW
    w4 = w.reshape(N_GROUPS, per, LRU_BW, LRU_BW)
    eye = jnp.eye(per, dtype=w.dtype)
    return jnp.einsum('gikn,ij->gikjn', w4, eye).reshape(N_GROUPS, GROUP, GROUP)


def _pack_rows(w):
    *lead, k, n = w.shape
    pairs = w.astype(jnp.bfloat16).reshape(*lead, k // 2, 2, n)
    return lax.bitcast_convert_type(jnp.swapaxes(pairs, -1, -2), jnp.uint32)


def _retention_tables(seq):
    f32 = jnp.float32
    half = RET_DK // 2
    freqs = ROPE_THETA ** (-jnp.arange(half, dtype=f32) / half)
    ang = jnp.arange(seq, dtype=f32)[:, None] * freqs[None, :]
    log_g = jnp.log1p(-(2.0 ** (-5.0 - jnp.arange(RET_HEADS, dtype=f32))))
    idx = jnp.arange(CHUNK, dtype=f32)
    diff = idx[:, None] - idx[None, :]
    inner = jnp.where(diff >= 0, jnp.exp(jnp.maximum(diff, 0.0)[None] * log_g[:, None, None]), 0.0)
    cross = jnp.exp((idx[:, None] + 1.0) * log_g[None, :])
    state = jnp.exp((CHUNK - 1.0 - idx[:, None]) * log_g[None, :])
    chunk = jnp.exp(CHUNK * log_g)[None, :]
    per_head = lambda m: jnp.broadcast_to(m.T[:, :, None], (RET_HEADS, CHUNK, LANES))
    k_scale = RET_DK ** -0.5
    assert k_scale == 2.0 ** -4
    return (jnp.cos(ang), jnp.sin(ang), inner * k_scale, per_head(cross),
            per_head(state * k_scale), jnp.repeat(chunk, RET_DV, axis=1))


def _const_spec(shape):
    zeros = (0,) * len(shape)
    return pl.BlockSpec(shape, lambda b, t: zeros, pipeline_mode=pl.Buffered(1))


def _layer(x, nin, win, cw, cb, gxw, gxb, gaw, gab, lam, gn, wpa, wpb, wout, nfin, tables,
           final_norm):
    B, S, D = x.shape
    assert D == D_MODEL and S % STEP_T == 0
    assert win.shape == (D, N_KINDS * D) and wpa.shape == wpb.shape == wout.shape == (D, D)
    bf16 = jnp.bfloat16
    row = lambda v: v.reshape(1, -1)
    gw = _pack_rows(0.5 * jnp.concatenate([_block_diag_groups(gxw), _block_diag_groups(gaw)], axis=-1))
    cos, sin, idec, cdec, sdec, chdec = tables
    operands = (
        x, row(nin), win, cw, row(cb), gw, row(gxb), row(gab), row(lam), row(gn),
        wpa, wpb, wout, row(nfin), cos, sin, idec, cdec, sdec, chdec)
    hbm_operands = (2, 10, 11, 12)
    tile_spec = pl.BlockSpec((1, STEP_T, D), lambda b, t: (b, t, 0))
    in_specs = []
    for i, o in enumerate(operands):
        if i == 0:
            in_specs.append(tile_spec)
        elif i in hbm_operands:
            in_specs.append(pl.BlockSpec(memory_space=pl.ANY))
        else:
            in_specs.append(_const_spec(o.shape))
    scratch = [
        pltpu.VMEM((D, N_KINDS * D), bf16),
        pltpu.VMEM((D, D), bf16),
        pltpu.VMEM((D, D), bf16),
        pltpu.VMEM((D, D), bf16),
        pltpu.VMEM((STAGE_SLOTS, D, STAGE_COLS), jnp.float32),
        pltpu.SemaphoreType.DMA((STAGE_SLOTS,)),
        pltpu.VMEM((TILE_T, D), bf16),
        pltpu.VMEM((D // LANES, TILE_T + HALO, LANES), jnp.float32),
        pltpu.VMEM((TILE_T, GROUP), jnp.float32),
        pltpu.VMEM((TILE_T, D), bf16),
        pltpu.VMEM((TILE_T, D), bf16),
        pltpu.VMEM((RET_HEADS, RET_DK, RET_DV), jnp.float32),
        pltpu.VMEM((SUBLANES, D), jnp.float32),
    ]
    return pl.pallas_call(
        functools.partial(_block_kernel, final_norm=final_norm),
        grid=(B, S // STEP_T),
        in_specs=in_specs,
        out_specs=tile_spec,
        out_shape=jax.ShapeDtypeStruct(x.shape, x.dtype),
        scratch_shapes=scratch,
        compiler_params=pltpu.CompilerParams(
            dimension_semantics=("arbitrary", "arbitrary"),
            vmem_limit_bytes=VMEM_LIMIT_BYTES),
        name="hybrid_block",
    )(*operands)


def kernel(x, norm_in, w_in, conv_w, conv_b, gate_x_w, gate_x_b, gate_a_w, gate_a_b, lru_lambda, gn_gain, w_proj_a, w_proj_b, w_out, norm_final):
    depth = w_in.shape[0]
    tables = _retention_tables(x.shape[1])
    for l in range(depth):
        x = _layer(x, norm_in[l], w_in[l], conv_w[l], conv_b[l], gate_x_w[l], gate_x_b[l],
                   gate_a_w[l], gate_a_b[l], lru_lambda[l], gn_gain[l].reshape(-1),
                   w_proj_a[l], w_proj_b[l], w_out[l], norm_final, tables,
                   final_norm=(l == depth - 1))
    return x
```

```python
import functools

import jax
import jax.numpy as jnp
from jax import lax
from jax.experimental import pallas as pl
from jax.experimental.pallas import tpu as pltpu

D_MODEL = 1024
LRU_BLOCKS = 16
LRU_BW = D_MODEL // LRU_BLOCKS
CONV_WIDTH = 4
LRU_C = 8.0
RET_HEADS = 4
RET_DK = 256
RET_DV = 256
CHUNK = 256
ROPE_THETA = 10000.0
EPS = 1e-6

KIND_XA, KIND_GA, KIND_Q, KIND_K, KIND_V, KIND_GB, KIND_MA, KIND_MB = range(8)
N_KINDS = 8
N_GROUP_KINDS = 6
HALVED_KINDS = (KIND_GA, KIND_GB, KIND_MA, KIND_MB)

LANES = 128
SUBLANES = 8
GROUP = 256
N_GROUPS = D_MODEL // GROUP
GROUP_COLS = N_GROUP_KINDS * GROUP
MERGE_COL0 = N_GROUPS * GROUP_COLS
TILE_T = 512
TILES_PER_STEP = 1
STEP_T = TILE_T * TILES_PER_STEP
N_CHUNKS = TILE_T // CHUNK
ROW_CHUNK = 32
OUT_ROWS = 512
FIRST_PART_KINDS = 3
STAGE_COLS = 256
STAGE_SLOTS = 4
PREP_ROWS = 64
HALO = SUBLANES
VMEM_LIMIT_BYTES = 56 * 1024 * 1024
TINY = 1e-37


def _dot(a, b):
    return jnp.dot(a, b, preferred_element_type=jnp.float32)


def _prepare_weights(win_hbm, wpa_hbm, wpb_hbm, wout_hbm, win_s, wpa_s, wpb_s, wout_s,
                     stage, sem):
    slabs = []
    for kind in range(N_KINDS):
        for c0 in range(0, D_MODEL, STAGE_COLS):
            src = win_hbm.at[:, pl.ds(kind * D_MODEL + c0, STAGE_COLS)]
            if kind < N_GROUP_KINDS:
                pieces = [(c - c0, (c // GROUP) * GROUP_COLS + kind * GROUP, GROUP)
                          for c in range(c0, c0 + STAGE_COLS, GROUP)]
            else:
                pieces = [(0, kind * D_MODEL + c0, STAGE_COLS)]
            slabs.append((src, win_s, pieces, kind in HALVED_KINDS))
    for src_hbm, dst, halve in ((wpa_hbm, wpa_s, True), (wpb_hbm, wpb_s, True),
                                (wout_hbm, wout_s, False)):
        for c0 in range(0, D_MODEL, STAGE_COLS):
            slabs.append((src_hbm.at[:, pl.ds(c0, STAGE_COLS)], dst,
                          [(0, c0, STAGE_COLS)], halve))

    def copy(n):
        slot = n % STAGE_SLOTS
        return pltpu.make_async_copy(slabs[n][0], stage.at[slot], sem.at[slot])

    for n in range(STAGE_SLOTS - 1):
        copy(n).start()
    for n, (_, dst, pieces, halve) in enumerate(slabs):
        if n + STAGE_SLOTS - 1 < len(slabs):
            copy(n + STAGE_SLOTS - 1).start()
        copy(n).wait()

        def convert(i, carry, slot=n % STAGE_SLOTS, dst=dst, pieces=pieces, halve=halve):
            rows = pl.ds(pl.multiple_of(i * PREP_ROWS, PREP_ROWS), PREP_ROWS)
            w = stage[slot, rows, :]
            if halve:
                w = 0.5 * w
            w = w.astype(jnp.bfloat16)
            for s0, d0, width in pieces:
                dst[rows, d0:d0 + width] = w[:, s0:s0 + width]
            return carry

        lax.fori_loop(0, D_MODEL // PREP_ROWS, convert, 0)


def _block_kernel(x_ref, nin_ref, win_hbm, cw_ref, cb_ref, gw_ref, gxb_ref, gab_ref, lam_ref,
                  gn_ref, wpa_hbm, wpb_hbm, wout_hbm, nfin_ref, cos_ref, sin_ref, idec_ref,
                  cdec_ref, sdec_ref, chdec_ref, out_ref,
                  win_s, wpa_s, wpb_s, wout_s, stage, sem,
                  h_ref, xa_ref, lru_ref, ya_ref, yb_ref, r_ref, hst_ref,
                  *, final_norm):
    @pl.when((pl.program_id(0) == 0) & (pl.program_id(1) == 0))
    def _():
        _prepare_weights(win_hbm, wpa_hbm, wpb_hbm, wout_hbm, win_s, wpa_s, wpb_s, wout_s,
                         stage, sem)

    def tile(i, row0):
        _tile_body(pl.program_id(1) * TILES_PER_STEP + i, row0,
                   x_ref, nin_ref, cw_ref, cb_ref, gw_ref, gxb_ref, gab_ref, lam_ref, gn_ref,
                   nfin_ref, cos_ref, sin_ref, idec_ref, cdec_ref, sdec_ref, chdec_ref, out_ref,
                   win_s, wpa_s, wpb_s, wout_s, h_ref, xa_ref, lru_ref, ya_ref, yb_ref, r_ref,
                   hst_ref, final_norm)

    for i in range(TILES_PER_STEP):
        tile(i, i * TILE_T)


def _tile_body(t, row0, x_ref, nin_ref, cw_ref, cb_ref, gw_ref, gxb_ref, gab_ref, lam_ref, gn_ref,
               nfin_ref, cos_ref, sin_ref, idec_ref, cdec_ref, sdec_ref, chdec_ref, out_ref,
               win_s, wpa_s, wpb_s, wout_s, h_ref, xa_ref, lru_ref, ya_ref, yb_ref, r_ref,
               hst_ref, final_norm):
    T = TILE_T
    bf16 = jnp.bfloat16

    @pl.when(t == 0)
    def _():
        r_ref[...] = jnp.zeros_like(r_ref)
        hst_ref[...] = jnp.zeros_like(hst_ref)
        xa_ref[:, 0:HALO, :] = jnp.zeros((D_MODEL // LANES, HALO, LANES), jnp.float32)

    g_in = nin_ref[...]
    for r0 in range(0, T, ROW_CHUNK):
        xr = x_ref[0, pl.ds(row0 + r0, ROW_CHUNK), :]
        ms = jnp.mean(xr * xr, axis=-1, keepdims=True)
        h_ref[r0:r0 + ROW_CHUNK, :] = (xr * lax.rsqrt(ms + EPS) * g_in).astype(bf16)

    n_rg = T // SUBLANES
    sub = lax.broadcasted_iota(jnp.int32, (n_rg, SUBLANES, GROUP), 1)
    half = RET_DK // 2
    both_halves = lambda m: jnp.concatenate([m, m], axis=1)

    part_kinds = ((0, FIRST_PART_KINDS), (FIRST_PART_KINDS, N_GROUP_KINDS))

    def project(g, part):
        lo, hi = part_kinds[part]
        return _dot(h_ref[...], win_s[:, g * GROUP_COLS + lo * GROUP:g * GROUP_COLS + hi * GROUP])

    def merge_gate(part):
        c0 = MERGE_COL0 + part * D_MODEL
        return _dot(h_ref[...], win_s[:, c0:c0 + D_MODEL])

    parts = [project(0, 0), project(0, 1)]
    for g in range(N_GROUPS):
        cs = slice(g * GROUP, (g + 1) * GROUP)
        def kind(i, p=tuple(parts)):
            part = int(i >= FIRST_PART_KINDS)
            j = i - part_kinds[part][0]
            return p[part][:, j * GROUP:(j + 1) * GROUP]
        if g + 1 < N_GROUPS:
            parts[0] = project(g + 1, 0)
        else:
            mg_a = merge_gate(0)

        xa = kind(KIND_XA)
        xc_blocks = []
        for b in range(GROUP // LANES):
            blk = g * (GROUP // LANES) + b
            lc = slice(blk * LANES, (blk + 1) * LANES)
            xa_b = xa[:, b * LANES:(b + 1) * LANES]
            xa_ref[blk, HALO:HALO + T, :] = xa_b
            xc_b = cb_ref[:, lc] + cw_ref[CONV_WIDTH - 1:CONV_WIDTH, lc] * xa_b
            for k in range(CONV_WIDTH - 1):
                lo = HALO - (CONV_WIDTH - 1) + k
                xc_b = xc_b + cw_ref[k:k + 1, lc] * xa_ref[blk, lo:lo + T, :]
            xa_ref[blk, 0:HALO, :] = xa_ref[blk, T:T + HALO, :]
            xc_blocks.append(xc_b)
        xc = jnp.concatenate(xc_blocks, axis=1)
        gates = _dot(xc.astype(bf16), pltpu.bitcast(gw_ref[g], bf16))

        q, k, v, gb = kind(KIND_Q), kind(KIND_K), kind(KIND_V), kind(KIND_GB)
        ga = kind(KIND_GA)
        pos = pl.ds(pl.multiple_of(t * T, T), T)
        cos = cos_ref[pos, :]
        sin = sin_ref[pos, :]
        q1, q2 = q[:, :half], q[:, half:]
        k1, k2 = k[:, :half], k[:, half:]
        qr = jnp.concatenate([q1 * cos - q2 * sin, q1 * sin + q2 * cos], axis=1).astype(bf16)
        kr = jnp.concatenate([k1 * cos - k2 * sin, k1 * sin + k2 * cos], axis=1).astype(bf16)
        v_bf = v.astype(bf16)
        scores, kv = [], []
        for c in range(N_CHUNKS):
            rows = slice(c * CHUNK, (c + 1) * CHUNK)
            scores.append(lax.dot_general(qr[rows], kr[rows], (((1,), (1,)), ((), ())),
                                          preferred_element_type=jnp.float32))
            kv.append(lax.dot_general(kr[rows], (v[rows] * both_halves(sdec_ref[g])).astype(bf16),
                                      (((0,), (0,)), ((), ())),
                                      preferred_element_type=jnp.float32))

        if g + 1 < N_GROUPS:
            parts[1] = project(g + 1, 1)
        else:
            mg_b = merge_gate(1)

        ti1 = 1.0 + jnp.tanh(gates[:, :GROUP] + 0.5 * gxb_ref[:, cs])
        tr1 = 1.0 + jnp.tanh(gates[:, GROUP:] + 0.5 * gab_ref[:, cs])
        nlam = -lam_ref[:, cs]
        softplus = jnp.maximum(nlam, 0.0) + jnp.log1p(jnp.exp(-jnp.abs(nlam)))
        nla = tr1 * ((0.5 * LRU_C) * softplus)
        a = jnp.exp(-nla)
        x4 = jnp.tanh(nla) * (a * (0.25 * a) + 0.25)
        mult = x4 * lax.rsqrt(jnp.maximum(x4, TINY))
        u = mult * (ti1 * xc)

        a3 = a.reshape(n_rg, SUBLANES, GROUP)
        u3 = u.reshape(n_rg, SUBLANES, GROUP)
        for s in (1, 2, 4):
            valid = sub >= s
            u_sh = jnp.where(valid, pltpu.roll(u3, s, 1), 0.0)
            a_sh = jnp.where(valid, pltpu.roll(a3, s, 1), 1.0)
            u3 = u3 + a3 * u_sh
            a3 = a3 * a_sh
        carry = hst_ref[:, cs]
        for j in range(n_rg):
            hj = u3[j] + a3[j] * carry
            lru_ref[j * SUBLANES:(j + 1) * SUBLANES, :] = hj
            carry = jnp.broadcast_to(hj[SUBLANES - 1:SUBLANES, :], (SUBLANES, GROUP))
        hst_ref[:, cs] = carry
        ya_ref[:, cs] = (ga * (1.0 + jnp.tanh(ga)) * lru_ref[...]).astype(bf16)

        silu_gb = gb * (1.0 + jnp.tanh(gb))
        gain = gn_ref[:, cs]
        r_state = r_ref[g]
        for c in range(N_CHUNKS):
            rows = slice(c * CHUNK, (c + 1) * CHUNK)
            cross = _dot(qr[rows], r_state.astype(bf16)) * both_halves(cdec_ref[g])
            r_state = chdec_ref[:, cs] * r_state + kv[c]
            inner = _dot((scores[c] * idec_ref[g]).astype(bf16), v_bf[rows])
            y = inner + cross
            mu = jnp.mean(y, axis=-1, keepdims=True)
            yc = y - mu
            var = jnp.mean(yc * yc, axis=-1, keepdims=True)
            gn = yc * lax.rsqrt(var + EPS) * gain
            yb_ref[rows, cs] = (silu_gb[rows] * gn).astype(bf16)
        r_ref[g] = r_state

    oa = _dot(ya_ref[...], wpa_s[...])
    ob = _dot(yb_ref[...], wpb_s[...])
    merged = ((1.0 + jnp.tanh(mg_a)) * oa + (1.0 + jnp.tanh(mg_b)) * ob).astype(bf16)
    g_fin = nfin_ref[...]
    for m0 in range(0, T, OUT_ROWS):
        out_ref[0, pl.ds(row0 + m0, OUT_ROWS), :] = _dot(merged[m0:m0 + OUT_ROWS], wout_s[...])
        for r0 in range(m0, m0 + OUT_ROWS, ROW_CHUNK):
            rows = pl.ds(row0 + r0, ROW_CHUNK)
            xo = x_ref[0, rows, :] + out_ref[0, rows, :]
            if final_norm:
                ms = jnp.mean(xo * xo, axis=-1, keepdims=True)
                xo = xo * lax.rsqrt(ms + EPS) * g_fin
            out_ref[0, rows, :] = xo


def _block_diag_groups(w):
    per = GROUP // LRU_BW
    w4 = w.reshape(N_GROUPS, per, LRU_BW, LRU_BW)
    eye = jnp.eye(per, dtype=w.dtype)
    return jnp.einsum('gikn,ij->gikjn', w4, eye).reshape(N_GROUPS, GROUP, GROUP)


def _pack_rows(w):
    *lead, k, n = w.shape
    pairs = w.astype(jnp.bfloat16).reshape(*lead, k // 2, 2, n)
    return lax.bitcast_convert_type(jnp.swapaxes(pairs, -1, -2), jnp.uint32)


def _retention_tables(seq):
    f32 = jnp.float32
    half = RET_DK // 2
    freqs = ROPE_THETA ** (-jnp.arange(half, dtype=f32) / half)
    ang = jnp.arange(seq, dtype=f32)[:, None] * freqs[None, :]
    log_g = jnp.log1p(-(2.0 ** (-5.0 - jnp.arange(RET_HEADS, dtype=f32))))
    idx = jnp.arange(CHUNK, dtype=f32)
    diff = idx[:, None] - idx[None, :]
    inner = jnp.where(diff >= 0, jnp.exp(jnp.maximum(diff, 0.0)[None] * log_g[:, None, None]), 0.0)
    cross = jnp.exp((idx[:, None] + 1.0) * log_g[None, :])
    state = jnp.exp((CHUNK - 1.0 - idx[:, None]) * log_g[None, :])
    chunk = jnp.exp(CHUNK * log_g)[None, :]
    per_head = lambda m: jnp.broadcast_to(m.T[:, :, None], (RET_HEADS, CHUNK, LANES))
    k_scale = RET_DK ** -0.5
    assert k_scale == 2.0 ** -4
    return (jnp.cos(ang), jnp.sin(ang), inner * k_scale, per_head(cross),
            per_head(state * k_scale), jnp.repeat(chunk, RET_DV, axis=1))


def _const_spec(shape):
    zeros = (0,) * len(shape)
    return pl.BlockSpec(shape, lambda b, t: zeros, pipeline_mode=pl.Buffered(1))


def _layer(x, nin, win, cw, cb, gxw, gxb, gaw, gab, lam, gn, wpa, wpb, wout, nfin, tables,
           final_norm):
    B, S, D = x.shape
    assert D == D_MODEL and S % STEP_T == 0
    assert win.shape == (D, N_KINDS * D) and wpa.shape == wpb.shape == wout.shape == (D, D)
    bf16 = jnp.bfloat16
    row = lambda v: v.reshape(1, -1)
    gw = _pack_rows(0.5 * jnp.concatenate([_block_diag_groups(gxw), _block_diag_groups(gaw)], axis=-1))
    cos, sin, idec, cdec, sdec, chdec = tables
    operands = (
        x, row(nin), win, cw, row(cb), gw, row(gxb), row(gab), row(lam), row(gn),
        wpa, wpb, wout, row(nfin), cos, sin, idec, cdec, sdec, chdec)
    hbm_operands = (2, 10, 11, 12)
    tile_spec = pl.BlockSpec((1, STEP_T, D), lambda b, t: (b, t, 0))
    in_specs = []
    for i, o in enumerate(operands):
        if i == 0:
            in_specs.append(tile_spec)
        elif i in hbm_operands:
            in_specs.append(pl.BlockSpec(memory_space=pl.ANY))
        else:
            in_specs.append(_const_spec(o.shape))
    scratch = [
        pltpu.VMEM((D, N_KINDS * D), bf16),
        pltpu.VMEM((D, D), bf16),
        pltpu.VMEM((D, D), bf16),
        pltpu.VMEM((D, D), bf16),
        pltpu.VMEM((STAGE_SLOTS, D, STAGE_COLS), jnp.float32),
        pltpu.SemaphoreType.DMA((STAGE_SLOTS,)),
        pltpu.VMEM((TILE_T, D), bf16),
        pltpu.VMEM((D // LANES, TILE_T + HALO, LANES), jnp.float32),
        pltpu.VMEM((TILE_T, GROUP), jnp.float32),
        pltpu.VMEM((TILE_T, D), bf16),
        pltpu.VMEM((TILE_T, D), bf16),
        pltpu.VMEM((RET_HEADS, RET_DK, RET_DV), jnp.float32),
        pltpu.VMEM((SUBLANES, D), jnp.float32),
    ]
    return pl.pallas_call(
        functools.partial(_block_kernel, final_norm=final_norm),
        grid=(B, S // STEP_T),
        in_specs=in_specs,
        out_specs=tile_spec,
        out_shape=jax.ShapeDtypeStruct(x.shape, x.dtype),
        scratch_shapes=scratch,
        compiler_params=pltpu.CompilerParams(
            dimension_semantics=("arbitrary", "arbitrary"),
            vmem_limit_bytes=VMEM_LIMIT_BYTES),
        name="hybrid_block",
    )(*operands)


def kernel(x, norm_in, w_in, conv_w, conv_b, gate_x_w, gate_x_b, gate_a_w, gate_a_b, lru_lambda, gn_gain, w_proj_a, w_proj_b, w_out, norm_final):
    depth = w_in.shape[0]
    tables = _retention_tables(x.shape[1])
    for l in range(depth):
        x = _layer(x, norm_in[l], w_in[l], conv_w[l], conv_b[l], gate_x_w[l], gate_x_b[l],
                   gate_a_w[l], gate_a_b[l], lru_lambda[l], gn_gain[l].reshape(-1),
                   w_proj_a[l], w_proj_b[l], w_out[l], norm_final, tables,
                   final_norm=(l == depth - 1))
    return x
```

```python
import functools

import jax
import jax.numpy as jnp
from jax import lax
from jax.experimental import pallas as pl
from jax.experimental.pallas import tpu as pltpu

D_MODEL = 1024
LRU_BLOCKS = 16
LRU_BW = D_MODEL // LRU_BLOCKS
CONV_WIDTH = 4
LRU_C = 8.0
RET_HEADS = 4
RET_DK = 256
RET_DV = 256
CHUNK = 256
ROPE_THETA = 10000.0
EPS = 1e-6

KIND_XA, KIND_GA, KIND_Q, KIND_K, KIND_V, KIND_GB, KIND_MA, KIND_MB = range(8)
N_KINDS = 8
N_GROUP_KINDS = 6
HALVED_KINDS = (KIND_GA, KIND_GB, KIND_MA, KIND_MB)

LANES = 128
SUBLANES = 8
GROUP = 256
N_GROUPS = D_MODEL // GROUP
GROUP_COLS = N_GROUP_KINDS * GROUP
MERGE_COL0 = N_GROUPS * GROUP_COLS
TILE_T = 512
TILES_PER_STEP = 1
STEP_T = TILE_T * TILES_PER_STEP
N_CHUNKS = TILE_T // CHUNK
ROW_CHUNK = 32
STAGE_COLS = 256
STAGE_SLOTS = 4
PREP_ROWS = 64
HALO = SUBLANES
VMEM_LIMIT_BYTES = 56 * 1024 * 1024
TINY = 1e-37


def _dot(a, b):
    return jnp.dot(a, b, preferred_element_type=jnp.float32)


def _prepare_weights(win_hbm, wpa_hbm, wpb_hbm, wout_hbm, win_s, wpa_s, wpb_s, wout_s,
                     stage, sem):
    slabs = []
    for kind in range(N_KINDS):
        for c0 in range(0, D_MODEL, STAGE_COLS):
            src = win_hbm.at[:, pl.ds(kind * D_MODEL + c0, STAGE_COLS)]
            if kind < N_GROUP_KINDS:
                pieces = [(c - c0, (c // GROUP) * GROUP_COLS + kind * GROUP, GROUP)
                          for c in range(c0, c0 + STAGE_COLS, GROUP)]
            else:
                pieces = [(0, kind * D_MODEL + c0, STAGE_COLS)]
            slabs.append((src, win_s, pieces, kind in HALVED_KINDS))
    for src_hbm, dst, halve in ((wpa_hbm, wpa_s, True), (wpb_hbm, wpb_s, True),
                                (wout_hbm, wout_s, False)):
        for c0 in range(0, D_MODEL, STAGE_COLS):
            slabs.append((src_hbm.at[:, pl.ds(c0, STAGE_COLS)], dst,
                          [(0, c0, STAGE_COLS)], halve))

    def copy(n):
        slot = n % STAGE_SLOTS
        return pltpu.make_async_copy(slabs[n][0], stage.at[slot], sem.at[slot])

    for n in range(STAGE_SLOTS - 1):
        copy(n).start()
    for n, (_, dst, pieces, halve) in enumerate(slabs):
        if n + STAGE_SLOTS - 1 < len(slabs):
            copy(n + STAGE_SLOTS - 1).start()
        copy(n).wait()

        def convert(i, carry, slot=n % STAGE_SLOTS, dst=dst, pieces=pieces, halve=halve):
            rows = pl.ds(pl.multiple_of(i * PREP_ROWS, PREP_ROWS), PREP_ROWS)
            w = stage[slot, rows, :]
            if halve:
                w = 0.5 * w
            w = w.astype(jnp.bfloat16)
            for s0, d0, width in pieces:
                dst[rows, d0:d0 + width] = w[:, s0:s0 + width]
            return carry

        lax.fori_loop(0, D_MODEL // PREP_ROWS, convert, 0)


def _block_kernel(x_ref, nin_ref, win_hbm, cw_ref, cb_ref, gw_ref, gxb_ref, gab_ref, lam_ref,
                  gn_ref, wpa_hbm, wpb_hbm, wout_hbm, nfin_ref, cos_ref, sin_ref, idec_ref,
                  cdec_ref, sdec_ref, chdec_ref, out_ref,
                  win_s, wpa_s, wpb_s, wout_s, stage, sem,
                  h_ref, xa_ref, lru_ref, ya_ref, yb_ref, r_ref, hst_ref,
                  *, final_norm):
    @pl.when((pl.program_id(0) == 0) & (pl.program_id(1) == 0))
    def _():
        _prepare_weights(win_hbm, wpa_hbm, wpb_hbm, wout_hbm, win_s, wpa_s, wpb_s, wout_s,
                         stage, sem)

    def tile(i, row0):
        _tile_body(pl.program_id(1) * TILES_PER_STEP + i, row0,
                   x_ref, nin_ref, cw_ref, cb_ref, gw_ref, gxb_ref, gab_ref, lam_ref, gn_ref,
                   nfin_ref, cos_ref, sin_ref, idec_ref, cdec_ref, sdec_ref, chdec_ref, out_ref,
                   win_s, wpa_s, wpb_s, wout_s, h_ref, xa_ref, lru_ref, ya_ref, yb_ref, r_ref,
                   hst_ref, final_norm)

    for i in range(TILES_PER_STEP):
        tile(i, i * TILE_T)


def _tile_body(t, row0, x_ref, nin_ref, cw_ref, cb_ref, gw_ref, gxb_ref, gab_ref, lam_ref, gn_ref,
               nfin_ref, cos_ref, sin_ref, idec_ref, cdec_ref, sdec_ref, chdec_ref, out_ref,
               win_s, wpa_s, wpb_s, wout_s, h_ref, xa_ref, lru_ref, ya_ref, yb_ref, r_ref,
               hst_ref, final_norm):
    T = TILE_T
    bf16 = jnp.bfloat16

    @pl.when(t == 0)
    def _():
        r_ref[...] = jnp.zeros_like(r_ref)
        hst_ref[...] = jnp.zeros_like(hst_ref)
        xa_ref[:, 0:HALO, :] = jnp.zeros((D_MODEL // LANES, HALO, LANES), jnp.float32)

    g_in = nin_ref[...]
    for r0 in range(0, T, ROW_CHUNK):
        xr = x_ref[0, pl.ds(row0 + r0, ROW_CHUNK), :]
        ms = jnp.mean(xr * xr, axis=-1, keepdims=True)
        h_ref[r0:r0 + ROW_CHUNK, :] = (xr * lax.rsqrt(ms + EPS) * g_in).astype(bf16)

    n_rg = T // SUBLANES
    sub = lax.broadcasted_iota(jnp.int32, (n_rg, SUBLANES, GROUP), 1)
    half = RET_DK // 2
    both_halves = lambda m: jnp.concatenate([m, m], axis=1)

    def project(g):
        return _dot(h_ref[...], win_s[:, g * GROUP_COLS:(g + 1) * GROUP_COLS])

    proj = project(0)
    for g in range(N_GROUPS):
        cs = slice(g * GROUP, (g + 1) * GROUP)
        kind = lambda i, p=proj: p[:, i * GROUP:(i + 1) * GROUP]

        xa = kind(KIND_XA)
        xc_blocks = []
        for b in range(GROUP // LANES):
            blk = g * (GROUP // LANES) + b
            lc = slice(blk * LANES, (blk + 1) * LANES)
            xa_b = xa[:, b * LANES:(b + 1) * LANES]
            xa_ref[blk, HALO:HALO + T, :] = xa_b
            xc_b = cb_ref[:, lc] + cw_ref[CONV_WIDTH - 1:CONV_WIDTH, lc] * xa_b
            for k in range(CONV_WIDTH - 1):
                lo = HALO - (CONV_WIDTH - 1) + k
                xc_b = xc_b + cw_ref[k:k + 1, lc] * xa_ref[blk, lo:lo + T, :]
            xa_ref[blk, 0:HALO, :] = xa_ref[blk, T:T + HALO, :]
            xc_blocks.append(xc_b)
        xc = jnp.concatenate(xc_blocks, axis=1)
        gates = _dot(xc.astype(bf16), pltpu.bitcast(gw_ref[g], bf16))

        q, k, v, gb = kind(KIND_Q), kind(KIND_K), kind(KIND_V), kind(KIND_GB)
        ga = kind(KIND_GA)
        pos = pl.ds(pl.multiple_of(t * T, T), T)
        cos = cos_ref[pos, :]
        sin = sin_ref[pos, :]
        q1, q2 = q[:, :half], q[:, half:]
        k1, k2 = k[:, :half], k[:, half:]
        qr = jnp.concatenate([q1 * cos - q2 * sin, q1 * sin + q2 * cos], axis=1).astype(bf16)
        kr = jnp.concatenate([k1 * cos - k2 * sin, k1 * sin + k2 * cos], axis=1).astype(bf16)
        v_bf = v.astype(bf16)
        kr_t = kr.T
        scores, kv = [], []
        for c in range(N_CHUNKS):
            rows = slice(c * CHUNK, (c + 1) * CHUNK)
            scores.append(_dot(qr[rows], kr_t[:, rows]))
            kv.append(_dot(kr_t[:, rows], (v[rows] * both_halves(sdec_ref[g])).astype(bf16)))

        if g + 1 < N_GROUPS:
            proj = project(g + 1)
        else:
            mg = _dot(h_ref[...], win_s[:, MERGE_COL0:MERGE_COL0 + 2 * D_MODEL])

        ti1 = 1.0 + jnp.tanh(gates[:, :GROUP] + 0.5 * gxb_ref[:, cs])
        tr1 = 1.0 + jnp.tanh(gates[:, GROUP:] + 0.5 * gab_ref[:, cs])
        nlam = -lam_ref[:, cs]
        softplus = jnp.maximum(nlam, 0.0) + jnp.log1p(jnp.exp(-jnp.abs(nlam)))
        nla = tr1 * ((0.5 * LRU_C) * softplus)
        a = jnp.exp(-nla)
        x4 = jnp.tanh(nla) * (a * (0.25 * a) + 0.25)
        mult = x4 * lax.rsqrt(jnp.maximum(x4, TINY))
        u = mult * (ti1 * xc)

        a3 = a.reshape(n_rg, SUBLANES, GROUP)
        u3 = u.reshape(n_rg, SUBLANES, GROUP)
        for s in (1, 2, 4):
            valid = sub >= s
            u_sh = jnp.where(valid, pltpu.roll(u3, s, 1), 0.0)
            a_sh = jnp.where(valid, pltpu.roll(a3, s, 1), 1.0)
            u3 = u3 + a3 * u_sh
            a3 = a3 * a_sh
        carry = hst_ref[:, cs]
        for j in range(n_rg):
            hj = u3[j] + a3[j] * carry
            lru_ref[j * SUBLANES:(j + 1) * SUBLANES, :] = hj
            carry = jnp.broadcast_to(hj[SUBLANES - 1:SUBLANES, :], (SUBLANES, GROUP))
        hst_ref[:, cs] = carry
        ya_ref[:, cs] = (ga * (1.0 + jnp.tanh(ga)) * lru_ref[...]).astype(bf16)

        silu_gb = gb * (1.0 + jnp.tanh(gb))
        gain = gn_ref[:, cs]
        r_state = r_ref[g]
        for c in range(N_CHUNKS):
            rows = slice(c * CHUNK, (c + 1) * CHUNK)
            cross = _dot(qr[rows], r_state.astype(bf16)) * both_halves(cdec_ref[g])
            r_state = chdec_ref[:, cs] * r_state + kv[c]
            inner = _dot((scores[c] * idec_ref[g]).astype(bf16), v_bf[rows])
            y = inner + cross
            mu = jnp.mean(y, axis=-1, keepdims=True)
            yc = y - mu
            var = jnp.mean(yc * yc, axis=-1, keepdims=True)
            gn = yc * lax.rsqrt(var + EPS) * gain
            yb_ref[rows, cs] = (silu_gb[rows] * gn).astype(bf16)
        r_ref[g] = r_state

    oa = _dot(ya_ref[...], wpa_s[...])
    ob = _dot(yb_ref[...], wpb_s[...])
    merged = ((1.0 + jnp.tanh(mg[:, :D_MODEL])) * oa
              + (1.0 + jnp.tanh(mg[:, D_MODEL:])) * ob).astype(bf16)
    out_ref[0, pl.ds(row0, T), :] = _dot(merged, wout_s[...])
    g_fin = nfin_ref[...]
    for r0 in range(0, T, ROW_CHUNK):
        rows = pl.ds(row0 + r0, ROW_CHUNK)
        xo = x_ref[0, rows, :] + out_ref[0, rows, :]
        if final_norm:
            ms = jnp.mean(xo * xo, axis=-1, keepdims=True)
            xo = xo * lax.rsqrt(ms + EPS) * g_fin
        out_ref[0, rows, :] = xo


def _block_diag_groups(w):
    per = GROUP // LRU_BW
    w4 = w.reshape(N_GROUPS, per, LRU_BW, LRU_BW)
    eye = jnp.eye(per, dtype=w.dtype)
    return jnp.einsum('gikn,ij->gikjn', w4, eye).reshape(N_GROUPS, GROUP, GROUP)


def _pack_rows(w):
    *lead, k, n = w.shape
    pairs = w.astype(jnp.bfloat16).reshape(*lead, k // 2, 2, n)
    return lax.bitcast_convert_type(jnp.swapaxes(pairs, -1, -2), jnp.uint32)


def _retention_tables(seq):
    f32 = jnp.float32
    half = RET_DK // 2
    freqs = ROPE_THETA ** (-jnp.arange(half, dtype=f32) / half)
    ang = jnp.arange(seq, dtype=f32)[:, None] * freqs[None, :]
    log_g = jnp.log1p(-(2.0 ** (-5.0 - jnp.arange(RET_HEADS, dtype=f32))))
    idx = jnp.arange(CHUNK, dtype=f32)
    diff = idx[:, None] - idx[None, :]
    inner = jnp.where(diff >= 0, jnp.exp(jnp.maximum(diff, 0.0)[None] * log_g[:, None, None]), 0.0)
    cross = jnp.exp((idx[:, None] + 1.0) * log_g[None, :])
    state = jnp.exp((CHUNK - 1.0 - idx[:, None]) * log_g[None, :])
    chunk = jnp.exp(CHUNK * log_g)[None, :]
    per_head = lambda m: jnp.broadcast_to(m.T[:, :, None], (RET_HEADS, CHUNK, LANES))
    k_scale = RET_DK ** -0.5
    assert k_scale == 2.0 ** -4
    return (jnp.cos(ang), jnp.sin(ang), inner * k_scale, per_head(cross),
            per_head(state * k_scale), jnp.repeat(chunk, RET_DV, axis=1))


def _const_spec(shape):
    zeros = (0,) * len(shape)
    return pl.BlockSpec(shape, lambda b, t: zeros, pipeline_mode=pl.Buffered(1))


def _layer(x, nin, win, cw, cb, gxw, gxb, gaw, gab, lam, gn, wpa, wpb, wout, nfin, tables,
           final_norm):
    B, S, D = x.shape
    assert D == D_MODEL and S % STEP_T == 0
    assert win.shape == (D, N_KINDS * D) and wpa.shape == wpb.shape == wout.shape == (D, D)
    bf16 = jnp.bfloat16
    row = lambda v: v.reshape(1, -1)
    gw = _pack_rows(0.5 * jnp.concatenate([_block_diag_groups(gxw), _block_diag_groups(gaw)], axis=-1))
    cos, sin, idec, cdec, sdec, chdec = tables
    operands = (
        x, row(nin), win, cw, row(cb), gw, row(gxb), row(gab), row(lam), row(gn),
        wpa, wpb, wout, row(nfin), cos, sin, idec, cdec, sdec, chdec)
    hbm_operands = (2, 10, 11, 12)
    tile_spec = pl.BlockSpec((1, STEP_T, D), lambda b, t: (b, t, 0))
    in_specs = []
    for i, o in enumerate(operands):
        if i == 0:
            in_specs.append(tile_spec)
        elif i in hbm_operands:
            in_specs.append(pl.BlockSpec(memory_space=pl.ANY))
        else:
            in_specs.append(_const_spec(o.shape))
    scratch = [
        pltpu.VMEM((D, N_KINDS * D), bf16),
        pltpu.VMEM((D, D), bf16),
        pltpu.VMEM((D, D), bf16),
        pltpu.VMEM((D, D), bf16),
        pltpu.VMEM((STAGE_SLOTS, D, STAGE_COLS), jnp.float32),
        pltpu.SemaphoreType.DMA((STAGE_SLOTS,)),
        pltpu.VMEM((TILE_T, D), bf16),
        pltpu.VMEM((D // LANES, TILE_T + HALO, LANES), jnp.float32),
        pltpu.VMEM((TILE_T, GROUP), jnp.float32),
        pltpu.VMEM((TILE_T, D), bf16),
        pltpu.VMEM((TILE_T, D), bf16),
        pltpu.VMEM((RET_HEADS, RET_DK, RET_DV), jnp.float32),
        pltpu.VMEM((SUBLANES, D), jnp.float32),
    ]
    return pl.pallas_call(
        functools.partial(_block_kernel, final_norm=final_norm),
        grid=(B, S // STEP_T),
        in_specs=in_specs,
        out_specs=tile_spec,
        out_shape=jax.ShapeDtypeStruct(x.shape, x.dtype),
        scratch_shapes=scratch,
        compiler_params=pltpu.CompilerParams(
            dimension_semantics=("arbitrary", "arbitrary"),
            vmem_limit_bytes=VMEM_LIMIT_BYTES),
        name="hybrid_block",
    )(*operands)


def kernel(x, norm_in, w_in, conv_w, conv_b, gate_x_w, gate_x_b, gate_a_w, gate_a_b, lru_lambda, gn_gain, w_proj_a, w_proj_b, w_out, norm_final):
    depth = w_in.shape[0]
    tables = _retention_tables(x.shape[1])
    for l in range(depth):
        x = _layer(x, norm_in[l], w_in[l], conv_w[l], conv_b[l], gate_x_w[l], gate_x_b[l],
                   gate_a_w[l], gate_a_b[l], lru_lambda[l], gn_gain[l].reshape(-1),
                   w_proj_a[l], w_proj_b[l], w_out[l], norm_final, tables,
                   final_norm=(l == depth - 1))
    return x
```

```python
import functools

import jax
import jax.numpy as jnp
import numpy as np
from jax import lax
from jax.experimental import pallas as pl
from jax.experimental.pallas import tpu as pltpu

D_MODEL = 1024
LRU_BLOCKS = 16
LRU_BW = D_MODEL // LRU_BLOCKS
CONV_WIDTH = 4
LRU_C = 8.0
RET_HEADS = 4
RET_DK = 256
RET_DV = 256
CHUNK = 256
ROPE_THETA = 10000.0
EPS = 1e-6

KIND_XA, KIND_GA, KIND_Q, KIND_K, KIND_V, KIND_GB, KIND_MA, KIND_MB = range(8)
N_KINDS = 8
N_GROUP_KINDS = 6
HALVED_KINDS = (KIND_GA, KIND_GB, KIND_MA, KIND_MB)

LANES = 128
SUBLANES = 8
GROUP = 256
N_GROUPS = D_MODEL // GROUP
GROUP_COLS = N_GROUP_KINDS * GROUP
MERGE_COL0 = N_GROUPS * GROUP_COLS
TILE_T = 512
TILES_PER_STEP = 1
STEP_T = TILE_T * TILES_PER_STEP
N_CHUNKS = TILE_T // CHUNK
ROW_CHUNK = 32
STAGE_COLS = 256
STAGE_SLOTS = 4
PREP_ROWS = 64
HALO = SUBLANES
VMEM_LIMIT_BYTES = 56 * 1024 * 1024
TINY = 1e-37


def _dot(a, b):
    return jnp.dot(a, b, preferred_element_type=jnp.float32)


def _prepare_weights(win_hbm, wpa_hbm, wpb_hbm, wout_hbm, win_s, wpa_s, wpb_s, wout_s,
                     stage, sem):
    slabs = []
    for kind in range(N_KINDS):
        for c0 in range(0, D_MODEL, STAGE_COLS):
            src = win_hbm.at[:, pl.ds(kind * D_MODEL + c0, STAGE_COLS)]
            if kind < N_GROUP_KINDS:
                pieces = [(c - c0, (c // GROUP) * GROUP_COLS + kind * GROUP, GROUP)
                          for c in range(c0, c0 + STAGE_COLS, GROUP)]
            else:
                pieces = [(0, kind * D_MODEL + c0, STAGE_COLS)]
            slabs.append((src, win_s, pieces, kind in HALVED_KINDS))
    for src_hbm, dst, halve in ((wpa_hbm, wpa_s, True), (wpb_hbm, wpb_s, True),
                                (wout_hbm, wout_s, False)):
        for c0 in range(0, D_MODEL, STAGE_COLS):
            slabs.append((src_hbm.at[:, pl.ds(c0, STAGE_COLS)], dst,
                          [(0, c0, STAGE_COLS)], halve))

    def copy(n):
        slot = n % STAGE_SLOTS
        return pltpu.make_async_copy(slabs[n][0], stage.at[slot], sem.at[slot])

    for n in range(STAGE_SLOTS - 1):
        copy(n).start()
    for n, (_, dst, pieces, halve) in enumerate(slabs):
        if n + STAGE_SLOTS - 1 < len(slabs):
            copy(n + STAGE_SLOTS - 1).start()
        copy(n).wait()

        def convert(i, carry, slot=n % STAGE_SLOTS, dst=dst, pieces=pieces, halve=halve):
            rows = pl.ds(pl.multiple_of(i * PREP_ROWS, PREP_ROWS), PREP_ROWS)
            w = stage[slot, rows, :]
            if halve:
                w = 0.5 * w
            w = w.astype(jnp.bfloat16)
            for s0, d0, width in pieces:
                dst[rows, d0:d0 + width] = w[:, s0:s0 + width]
            return carry

        lax.fori_loop(0, D_MODEL // PREP_ROWS, convert, 0)


def _block_kernel(x_ref, nin_ref, win_hbm, cw_ref, cb_ref, gw_ref, gxb_ref, gab_ref, lam_ref,
                  gn_ref, wpa_hbm, wpb_hbm, wout_hbm, nfin_ref, cos_ref, sin_ref, idec_ref,
                  cdec_ref, sdec_ref, chdec_ref, out_ref,
                  win_s, wpa_s, wpb_s, wout_s, stage, sem,
                  h_ref, xa_ref, lru_ref, ya_ref, yb_ref, r_ref, hst_ref,
                  *, final_norm):
    @pl.when((pl.program_id(0) == 0) & (pl.program_id(1) == 0))
    def _():
        _prepare_weights(win_hbm, wpa_hbm, wpb_hbm, wout_hbm, win_s, wpa_s, wpb_s, wout_s,
                         stage, sem)

    def tile(i, row0):
        _tile_body(pl.program_id(1) * TILES_PER_STEP + i, row0,
                   x_ref, nin_ref, cw_ref, cb_ref, gw_ref, gxb_ref, gab_ref, lam_ref, gn_ref,
                   nfin_ref, cos_ref, sin_ref, idec_ref, cdec_ref, sdec_ref, chdec_ref, out_ref,
                   win_s, wpa_s, wpb_s, wout_s, h_ref, xa_ref, lru_ref, ya_ref, yb_ref, r_ref,
                   hst_ref, final_norm)

    for i in range(TILES_PER_STEP):
        tile(i, i * TILE_T)


def _tile_body(t, row0, x_ref, nin_ref, cw_ref, cb_ref, gw_ref, gxb_ref, gab_ref, lam_ref, gn_ref,
               nfin_ref, cos_ref, sin_ref, idec_ref, cdec_ref, sdec_ref, chdec_ref, out_ref,
               win_s, wpa_s, wpb_s, wout_s, h_ref, xa_ref, lru_ref, ya_ref, yb_ref, r_ref,
               hst_ref, final_norm):
    T = TILE_T
    bf16 = jnp.bfloat16

    @pl.when(t == 0)
    def _():
        r_ref[...] = jnp.zeros_like(r_ref)
        hst_ref[...] = jnp.zeros_like(hst_ref)
        xa_ref[:, 0:HALO, :] = jnp.zeros((D_MODEL // LANES, HALO, LANES), jnp.float32)

    g_in = nin_ref[...]
    for r0 in range(0, T, ROW_CHUNK):
        xr = x_ref[0, pl.ds(row0 + r0, ROW_CHUNK), :]
        ms = jnp.mean(xr * xr, axis=-1, keepdims=True)
        h_ref[r0:r0 + ROW_CHUNK, :] = (xr * lax.rsqrt(ms + EPS) * g_in).astype(bf16)

    n_rg = T // SUBLANES
    sub = lax.broadcasted_iota(jnp.int32, (n_rg, SUBLANES, GROUP), 1)
    half = RET_DK // 2
    both_halves = lambda m: jnp.concatenate([m, m], axis=1)

    def project(g):
        return _dot(h_ref[...], win_s[:, g * GROUP_COLS:(g + 1) * GROUP_COLS])

    proj = project(0)
    for g in range(N_GROUPS):
        cs = slice(g * GROUP, (g + 1) * GROUP)
        kind = lambda i, p=proj: p[:, i * GROUP:(i + 1) * GROUP]

        xa = kind(KIND_XA)
        xc_blocks = []
        for b in range(GROUP // LANES):
            blk = g * (GROUP // LANES) + b
            lc = slice(blk * LANES, (blk + 1) * LANES)
            xa_b = xa[:, b * LANES:(b + 1) * LANES]
            xa_ref[blk, HALO:HALO + T, :] = xa_b
            xc_b = cb_ref[:, lc] + cw_ref[CONV_WIDTH - 1:CONV_WIDTH, lc] * xa_b
            for k in range(CONV_WIDTH - 1):
                lo = HALO - (CONV_WIDTH - 1) + k
                xc_b = xc_b + cw_ref[k:k + 1, lc] * xa_ref[blk, lo:lo + T, :]
            xa_ref[blk, 0:HALO, :] = xa_ref[blk, T:T + HALO, :]
            xc_blocks.append(xc_b)
        xc = jnp.concatenate(xc_blocks, axis=1)
        gates = _dot(xc.astype(bf16), pltpu.bitcast(gw_ref[g], bf16))

        q, k, v, gb = kind(KIND_Q), kind(KIND_K), kind(KIND_V), kind(KIND_GB)
        ga = kind(KIND_GA)
        pos = pl.ds(pl.multiple_of(t * T, T), T)
        cos = cos_ref[pos, :]
        sin = sin_ref[pos, :]
        q1, q2 = q[:, :half], q[:, half:]
        k1, k2 = k[:, :half], k[:, half:]
        qr = jnp.concatenate([q1 * cos - q2 * sin, q1 * sin + q2 * cos], axis=1).astype(bf16)
        kr = jnp.concatenate([k1 * cos - k2 * sin, k1 * sin + k2 * cos], axis=1).astype(bf16)
        v_bf = v.astype(bf16)
        scores, kv = [], []
        for c in range(N_CHUNKS):
            rows = slice(c * CHUNK, (c + 1) * CHUNK)
            scores.append(lax.dot_general(qr[rows], kr[rows], (((1,), (1,)), ((), ())),
                                          preferred_element_type=jnp.float32))
            kv.append(lax.dot_general(kr[rows], (v[rows] * both_halves(sdec_ref[g])).astype(bf16),
                                      (((0,), (0,)), ((), ())),
                                      preferred_element_type=jnp.float32))

        if g + 1 < N_GROUPS:
            proj = project(g + 1)
        else:
            mg = _dot(h_ref[...], win_s[:, MERGE_COL0:MERGE_COL0 + 2 * D_MODEL])

        ti1 = 1.0 + jnp.tanh(gates[:, :GROUP] + 0.5 * gxb_ref[:, cs])
        tr1 = 1.0 + jnp.tanh(gates[:, GROUP:] + 0.5 * gab_ref[:, cs])
        nlam = -lam_ref[:, cs]
        softplus = jnp.maximum(nlam, 0.0) + jnp.log1p(jnp.exp(-jnp.abs(nlam)))
        nla = tr1 * ((0.5 * LRU_C) * softplus)
        a = jnp.exp(-nla)
        x4 = jnp.tanh(nla) * (a * (0.25 * a) + 0.25)
        mult = x4 * lax.rsqrt(jnp.maximum(x4, TINY))
        u = mult * (ti1 * xc)

        a3 = a.reshape(n_rg, SUBLANES, GROUP)
        u3 = u.reshape(n_rg, SUBLANES, GROUP)
        for s in (1, 2, 4):
            valid = sub >= s
            u_sh = jnp.where(valid, pltpu.roll(u3, s, 1), 0.0)
            a_sh = jnp.where(valid, pltpu.roll(a3, s, 1), 1.0)
            u3 = u3 + a3 * u_sh
            a3 = a3 * a_sh
        carry = hst_ref[:, cs]
        for j in range(n_rg):
            hj = u3[j] + a3[j] * carry
            lru_ref[j * SUBLANES:(j + 1) * SUBLANES, :] = hj
            carry = jnp.broadcast_to(hj[SUBLANES - 1:SUBLANES, :], (SUBLANES, GROUP))
        hst_ref[:, cs] = carry
        ya_ref[:, cs] = (ga * (1.0 + jnp.tanh(ga)) * lru_ref[...]).astype(bf16)

        silu_gb = gb * (1.0 + jnp.tanh(gb))
        gain = gn_ref[:, cs]
        r_state = r_ref[g]
        for c in range(N_CHUNKS):
            rows = slice(c * CHUNK, (c + 1) * CHUNK)
            cross = _dot(qr[rows], r_state.astype(bf16)) * both_halves(cdec_ref[g])
            r_state = chdec_ref[:, cs] * r_state + kv[c]
            inner = _dot((scores[c] * idec_ref[g]).astype(bf16), v_bf[rows])
            y = inner + cross
            mu = jnp.mean(y, axis=-1, keepdims=True)
            yc = y - mu
            var = jnp.mean(yc * yc, axis=-1, keepdims=True)
            gn = yc * lax.rsqrt(var + EPS) * gain
            yb_ref[rows, cs] = (silu_gb[rows] * gn).astype(bf16)
        r_ref[g] = r_state

    oa = _dot(ya_ref[...], wpa_s[...])
    ob = _dot(yb_ref[...], wpb_s[...])
    merged = ((1.0 + jnp.tanh(mg[:, :D_MODEL])) * oa
              + (1.0 + jnp.tanh(mg[:, D_MODEL:])) * ob).astype(bf16)
    out_ref[0, pl.ds(row0, T), :] = _dot(merged, wout_s[...])
    g_fin = nfin_ref[...]
    for r0 in range(0, T, ROW_CHUNK):
        rows = pl.ds(row0 + r0, ROW_CHUNK)
        xo = x_ref[0, rows, :] + out_ref[0, rows, :]
        if final_norm:
            ms = jnp.mean(xo * xo, axis=-1, keepdims=True)
            xo = xo * lax.rsqrt(ms + EPS) * g_fin
        out_ref[0, rows, :] = xo


def _block_diag_groups(w):
    per = GROUP // LRU_BW
    w4 = w.reshape(N_GROUPS, per, LRU_BW, LRU_BW)
    eye = jnp.eye(per, dtype=w.dtype)
    return jnp.einsum('gikn,ij->gikjn', w4, eye).reshape(N_GROUPS, GROUP, GROUP)


def _pack_rows(w):
    *lead, k, n = w.shape
    pairs = w.astype(jnp.bfloat16).reshape(*lead, k // 2, 2, n)
    return lax.bitcast_convert_type(jnp.swapaxes(pairs, -1, -2), jnp.uint32)


def _retention_tables(seq):
    half = RET_DK // 2
    freqs = ROPE_THETA ** (-np.arange(half, dtype=np.float64) / half)
    ang = np.arange(seq, dtype=np.float64)[:, None] * freqs[None, :]
    log_g = np.log1p(-(2.0 ** (-5.0 - np.arange(RET_HEADS, dtype=np.float64))))
    idx = np.arange(CHUNK, dtype=np.float64)
    diff = idx[:, None] - idx[None, :]
    inner = np.where(diff >= 0, np.exp(np.maximum(diff, 0.0)[None] * log_g[:, None, None]), 0.0)
    cross = np.exp((idx[:, None] + 1.0) * log_g[None, :])
    state = np.exp((CHUNK - 1.0 - idx[:, None]) * log_g[None, :])
    chunk = np.exp(CHUNK * log_g)[None, :]
    per_head = lambda m: np.broadcast_to(m.T[:, :, None], (RET_HEADS, CHUNK, LANES))
    k_scale = RET_DK ** -0.5
    assert k_scale == 2.0 ** -4
    tables = (np.cos(ang), np.sin(ang), inner * k_scale, per_head(cross),
              per_head(state * k_scale), np.repeat(chunk, RET_DV, axis=1))
    return tuple(jnp.asarray(np.ascontiguousarray(t), dtype=jnp.float32) for t in tables)


def _const_spec(shape):
    zeros = (0,) * len(shape)
    return pl.BlockSpec(shape, lambda b, t: zeros, pipeline_mode=pl.Buffered(1))


def _layer(x, nin, win, cw, cb, gxw, gxb, gaw, gab, lam, gn, wpa, wpb, wout, nfin, tables,
           final_norm):
    B, S, D = x.shape
    assert D == D_MODEL and S % STEP_T == 0
    assert win.shape == (D, N_KINDS * D) and wpa.shape == wpb.shape == wout.shape == (D, D)
    bf16 = jnp.bfloat16
    row = lambda v: v.reshape(1, -1)
    gw = _pack_rows(0.5 * jnp.concatenate([_block_diag_groups(gxw), _block_diag_groups(gaw)], axis=-1))
    cos, sin, idec, cdec, sdec, chdec = tables
    operands = (
        x, row(nin), win, cw, row(cb), gw, row(gxb), row(gab), row(lam), row(gn),
        wpa, wpb, wout, row(nfin), cos, sin, idec, cdec, sdec, chdec)
    hbm_operands = (2, 10, 11, 12)
    tile_spec = pl.BlockSpec((1, STEP_T, D), lambda b, t: (b, t, 0))
    in_specs = []
    for i, o in enumerate(operands):
        if i == 0:
            in_specs.append(tile_spec)
        elif i in hbm_operands:
            in_specs.append(pl.BlockSpec(memory_space=pl.ANY))
        else:
            in_specs.append(_const_spec(o.shape))
    scratch = [
        pltpu.VMEM((D, N_KINDS * D), bf16),
        pltpu.VMEM((D, D), bf16),
        pltpu.VMEM((D, D), bf16),
        pltpu.VMEM((D, D), bf16),
        pltpu.VMEM((STAGE_SLOTS, D, STAGE_COLS), jnp.float32),
        pltpu.SemaphoreType.DMA((STAGE_SLOTS,)),
        pltpu.VMEM((TILE_T, D), bf16),
        pltpu.VMEM((D // LANES, TILE_T + HALO, LANES), jnp.float32),
        pltpu.VMEM((TILE_T, GROUP), jnp.float32),
        pltpu.VMEM((TILE_T, D), bf16),
        pltpu.VMEM((TILE_T, D), bf16),
        pltpu.VMEM((RET_HEADS, RET_DK, RET_DV), jnp.float32),
        pltpu.VMEM((SUBLANES, D), jnp.float32),
    ]
    return pl.pallas_call(
        functools.partial(_block_kernel, final_norm=final_norm),
        grid=(B, S // STEP_T),
        in_specs=in_specs,
        out_specs=tile_spec,
        out_shape=jax.ShapeDtypeStruct(x.shape, x.dtype),
        scratch_shapes=scratch,
        compiler_params=pltpu.CompilerParams(
            dimension_semantics=("arbitrary", "arbitrary"),
            vmem_limit_bytes=VMEM_LIMIT_BYTES),
        name="hybrid_block",
    )(*operands)


def kernel(x, norm_in, w_in, conv_w, conv_b, gate_x_w, gate_x_b, gate_a_w, gate_a_b, lru_lambda, gn_gain, w_proj_a, w_proj_b, w_out, norm_final):
    depth = w_in.shape[0]
    tables = _retention_tables(x.shape[1])
    for l in range(depth):
        x = _layer(x, norm_in[l], w_in[l], conv_w[l], conv_b[l], gate_x_w[l], gate_x_b[l],
                   gate_a_w[l], gate_a_b[l], lru_lambda[l], gn_gain[l].reshape(-1),
                   w_proj_a[l], w_proj_b[l], w_out[l], norm_final, tables,
                   final_norm=(l == depth - 1))
    return x
```

```python
import functools

import jax
import jax.numpy as jnp
import numpy as np
from jax import lax
from jax.experimental import pallas as pl
from jax.experimental.pallas import tpu as pltpu

D_MODEL = 1024
LRU_BLOCKS = 16
LRU_BW = D_MODEL // LRU_BLOCKS
CONV_WIDTH = 4
LRU_C = 8.0
RET_HEADS = 4
RET_DK = 256
RET_DV = 256
CHUNK = 256
ROPE_THETA = 10000.0
EPS = 1e-6

KIND_XA, KIND_GA, KIND_Q, KIND_K, KIND_V, KIND_GB, KIND_MA, KIND_MB = range(8)
N_KINDS = 8
N_GROUP_KINDS = 6
HALVED_KINDS = (KIND_GA, KIND_GB, KIND_MA, KIND_MB)

LANES = 128
SUBLANES = 8
GROUP = 256
N_GROUPS = D_MODEL // GROUP
GROUP_COLS = N_GROUP_KINDS * GROUP
MERGE_COL0 = N_GROUPS * GROUP_COLS
TILE_T = 512
TILES_PER_STEP = 2
STEP_T = TILE_T * TILES_PER_STEP
N_CHUNKS = TILE_T // CHUNK
ROW_CHUNK = 32
STAGE_COLS = 256
STAGE_SLOTS = 2
PREP_ROWS = 64
HALO = SUBLANES
VMEM_LIMIT_BYTES = 60 * 1024 * 1024
TINY = 1e-37


def _dot(a, b):
    return jnp.dot(a, b, preferred_element_type=jnp.float32)


def _prepare_weights(win_hbm, wpa_hbm, wpb_hbm, wout_hbm, win_s, wpa_s, wpb_s, wout_s,
                     stage, sem):
    slabs = []
    for kind in range(N_KINDS):
        for c0 in range(0, D_MODEL, STAGE_COLS):
            src = win_hbm.at[:, pl.ds(kind * D_MODEL + c0, STAGE_COLS)]
            if kind < N_GROUP_KINDS:
                pieces = [(c - c0, (c // GROUP) * GROUP_COLS + kind * GROUP, GROUP)
                          for c in range(c0, c0 + STAGE_COLS, GROUP)]
            else:
                pieces = [(0, kind * D_MODEL + c0, STAGE_COLS)]
            slabs.append((src, win_s, pieces, kind in HALVED_KINDS))
    for src_hbm, dst, halve in ((wpa_hbm, wpa_s, True), (wpb_hbm, wpb_s, True),
                                (wout_hbm, wout_s, False)):
        for c0 in range(0, D_MODEL, STAGE_COLS):
            slabs.append((src_hbm.at[:, pl.ds(c0, STAGE_COLS)], dst,
                          [(0, c0, STAGE_COLS)], halve))

    def copy(n):
        slot = n % STAGE_SLOTS
        return pltpu.make_async_copy(slabs[n][0], stage.at[slot], sem.at[slot])

    for n in range(STAGE_SLOTS - 1):
        copy(n).start()
    for n, (_, dst, pieces, halve) in enumerate(slabs):
        if n + STAGE_SLOTS - 1 < len(slabs):
            copy(n + STAGE_SLOTS - 1).start()
        copy(n).wait()

        def convert(i, carry, slot=n % STAGE_SLOTS, dst=dst, pieces=pieces, halve=halve):
            rows = pl.ds(pl.multiple_of(i * PREP_ROWS, PREP_ROWS), PREP_ROWS)
            w = stage[slot, rows, :]
            if halve:
                w = 0.5 * w
            w = w.astype(jnp.bfloat16)
            for s0, d0, width in pieces:
                dst[rows, d0:d0 + width] = w[:, s0:s0 + width]
            return carry

        lax.fori_loop(0, D_MODEL // PREP_ROWS, convert, 0)


def _block_kernel(x_ref, nin_ref, win_hbm, cw_ref, cb_ref, gw_ref, gxb_ref, gab_ref, lam_ref,
                  gn_ref, wpa_hbm, wpb_hbm, wout_hbm, nfin_ref, cos_ref, sin_ref, idec_ref,
                  cdec_ref, sdec_ref, chdec_ref, out_ref,
                  win_s, wpa_s, wpb_s, wout_s, stage, sem,
                  h_ref, xa_ref, lru_ref, ya_ref, yb_ref, r_ref, hst_ref,
                  *, final_norm):
    @pl.when((pl.program_id(0) == 0) & (pl.program_id(1) == 0))
    def _():
        _prepare_weights(win_hbm, wpa_hbm, wpb_hbm, wout_hbm, win_s, wpa_s, wpb_s, wout_s,
                         stage, sem)

    def tile(i, row0):
        _tile_body(pl.program_id(1) * TILES_PER_STEP + i, row0,
                   x_ref, nin_ref, cw_ref, cb_ref, gw_ref, gxb_ref, gab_ref, lam_ref, gn_ref,
                   nfin_ref, cos_ref, sin_ref, idec_ref, cdec_ref, sdec_ref, chdec_ref, out_ref,
                   win_s, wpa_s, wpb_s, wout_s, h_ref, xa_ref, lru_ref, ya_ref, yb_ref, r_ref,
                   hst_ref, final_norm)

    for i in range(TILES_PER_STEP):
        tile(i, i * TILE_T)


def _tile_body(t, row0, x_ref, nin_ref, cw_ref, cb_ref, gw_ref, gxb_ref, gab_ref, lam_ref, gn_ref,
               nfin_ref, cos_ref, sin_ref, idec_ref, cdec_ref, sdec_ref, chdec_ref, out_ref,
               win_s, wpa_s, wpb_s, wout_s, h_ref, xa_ref, lru_ref, ya_ref, yb_ref, r_ref,
               hst_ref, final_norm):
    T = TILE_T
    bf16 = jnp.bfloat16

    @pl.when(t == 0)
    def _():
        r_ref[...] = jnp.zeros_like(r_ref)
        hst_ref[...] = jnp.zeros_like(hst_ref)
        xa_ref[:, 0:HALO, :] = jnp.zeros((D_MODEL // LANES, HALO, LANES), jnp.float32)

    g_in = nin_ref[...]
    for r0 in range(0, T, ROW_CHUNK):
        xr = x_ref[0, pl.ds(row0 + r0, ROW_CHUNK), :]
        ms = jnp.mean(xr * xr, axis=-1, keepdims=True)
        h_ref[r0:r0 + ROW_CHUNK, :] = (xr * lax.rsqrt(ms + EPS) * g_in).astype(bf16)

    n_rg = T // SUBLANES
    sub = lax.broadcasted_iota(jnp.int32, (n_rg, SUBLANES, GROUP), 1)
    half = RET_DK // 2
    both_halves = lambda m: jnp.concatenate([m, m], axis=1)

    def project(g):
        return _dot(h_ref[...], win_s[:, g * GROUP_COLS:(g + 1) * GROUP_COLS])

    proj = project(0)
    for g in range(N_GROUPS):
        cs = slice(g * GROUP, (g + 1) * GROUP)
        kind = lambda i, p=proj: p[:, i * GROUP:(i + 1) * GROUP]

        xa = kind(KIND_XA)
        xc_blocks = []
        for b in range(GROUP // LANES):
            blk = g * (GROUP // LANES) + b
            lc = slice(blk * LANES, (blk + 1) * LANES)
            xa_b = xa[:, b * LANES:(b + 1) * LANES]
            xa_ref[blk, HALO:HALO + T, :] = xa_b
            xc_b = cb_ref[:, lc] + cw_ref[CONV_WIDTH - 1:CONV_WIDTH, lc] * xa_b
            for k in range(CONV_WIDTH - 1):
                lo = HALO - (CONV_WIDTH - 1) + k
                xc_b = xc_b + cw_ref[k:k + 1, lc] * xa_ref[blk, lo:lo + T, :]
            xa_ref[blk, 0:HALO, :] = xa_ref[blk, T:T + HALO, :]
            xc_blocks.append(xc_b)
        xc = jnp.concatenate(xc_blocks, axis=1)
        gates = _dot(xc.astype(bf16), pltpu.bitcast(gw_ref[g], bf16))

        q, k, v, gb = kind(KIND_Q), kind(KIND_K), kind(KIND_V), kind(KIND_GB)
        ga = kind(KIND_GA)
        pos = pl.ds(pl.multiple_of(t * T, T), T)
        cos = cos_ref[pos, :]
        sin = sin_ref[pos, :]
        q1, q2 = q[:, :half], q[:, half:]
        k1, k2 = k[:, :half], k[:, half:]
        qr = jnp.concatenate([q1 * cos - q2 * sin, q1 * sin + q2 * cos], axis=1).astype(bf16)
        kr = jnp.concatenate([k1 * cos - k2 * sin, k1 * sin + k2 * cos], axis=1).astype(bf16)
        v_bf = v.astype(bf16)
        scores, kv = [], []
        for c in range(N_CHUNKS):
            rows = slice(c * CHUNK, (c + 1) * CHUNK)
            scores.append(lax.dot_general(qr[rows], kr[rows], (((1,), (1,)), ((), ())),
                                          preferred_element_type=jnp.float32))
            kv.append(lax.dot_general(kr[rows], (v[rows] * both_halves(sdec_ref[g])).astype(bf16),
                                      (((0,), (0,)), ((), ())),
                                      preferred_element_type=jnp.float32))

        if g + 1 < N_GROUPS:
            proj = project(g + 1)
        else:
            mg = _dot(h_ref[...], win_s[:, MERGE_COL0:MERGE_COL0 + 2 * D_MODEL])

        ti1 = 1.0 + jnp.tanh(gates[:, :GROUP] + 0.5 * gxb_ref[:, cs])
        tr1 = 1.0 + jnp.tanh(gates[:, GROUP:] + 0.5 * gab_ref[:, cs])
        nlam = -lam_ref[:, cs]
        softplus = jnp.maximum(nlam, 0.0) + jnp.log1p(jnp.exp(-jnp.abs(nlam)))
        nla = tr1 * ((0.5 * LRU_C) * softplus)
        a = jnp.exp(-nla)
        x4 = jnp.tanh(nla) * (a * (0.25 * a) + 0.25)
        mult = x4 * lax.rsqrt(jnp.maximum(x4, TINY))
        u = mult * (ti1 * xc)

        a3 = a.reshape(n_rg, SUBLANES, GROUP)
        u3 = u.reshape(n_rg, SUBLANES, GROUP)
        for s in (1, 2, 4):
            valid = sub >= s
            u_sh = jnp.where(valid, pltpu.roll(u3, s, 1), 0.0)
            a_sh = jnp.where(valid, pltpu.roll(a3, s, 1), 1.0)
            u3 = u3 + a3 * u_sh
            a3 = a3 * a_sh
        carry = hst_ref[:, cs]
        for j in range(n_rg):
            hj = u3[j] + a3[j] * carry
            lru_ref[j * SUBLANES:(j + 1) * SUBLANES, :] = hj
            carry = jnp.broadcast_to(hj[SUBLANES - 1:SUBLANES, :], (SUBLANES, GROUP))
        hst_ref[:, cs] = carry
        ya_ref[:, cs] = (ga * (1.0 + jnp.tanh(ga)) * lru_ref[...]).astype(bf16)

        silu_gb = gb * (1.0 + jnp.tanh(gb))
        gain = gn_ref[:, cs]
        r_state = r_ref[g]
        for c in range(N_CHUNKS):
            rows = slice(c * CHUNK, (c + 1) * CHUNK)
            cross = _dot(qr[rows], r_state.astype(bf16)) * both_halves(cdec_ref[g])
            r_state = chdec_ref[:, cs] * r_state + kv[c]
            inner = _dot((scores[c] * idec_ref[g]).astype(bf16), v_bf[rows])
            y = inner + cross
            mu = jnp.mean(y, axis=-1, keepdims=True)
            yc = y - mu
            var = jnp.mean(yc * yc, axis=-1, keepdims=True)
            gn = yc * lax.rsqrt(var + EPS) * gain
            yb_ref[rows, cs] = (silu_gb[rows] * gn).astype(bf16)
        r_ref[g] = r_state

    oa = _dot(ya_ref[...], wpa_s[...])
    ob = _dot(yb_ref[...], wpb_s[...])
    merged = ((1.0 + jnp.tanh(mg[:, :D_MODEL])) * oa
              + (1.0 + jnp.tanh(mg[:, D_MODEL:])) * ob).astype(bf16)
    out_ref[0, pl.ds(row0, T), :] = _dot(merged, wout_s[...])
    g_fin = nfin_ref[...]
    for r0 in range(0, T, ROW_CHUNK):
        rows = pl.ds(row0 + r0, ROW_CHUNK)
        xo = x_ref[0, rows, :] + out_ref[0, rows, :]
        if final_norm:
            ms = jnp.mean(xo * xo, axis=-1, keepdims=True)
            xo = xo * lax.rsqrt(ms + EPS) * g_fin
        out_ref[0, rows, :] = xo


def _block_diag_groups(w):
    per = GROUP // LRU_BW
    w4 = w.reshape(N_GROUPS, per, LRU_BW, LRU_BW)
    eye = jnp.eye(per, dtype=w.dtype)
    return jnp.einsum('gikn,ij->gikjn', w4, eye).reshape(N_GROUPS, GROUP, GROUP)


def _pack_rows(w):
    *lead, k, n = w.shape
    pairs = w.astype(jnp.bfloat16).reshape(*lead, k // 2, 2, n)
    return lax.bitcast_convert_type(jnp.swapaxes(pairs, -1, -2), jnp.uint32)


def _retention_tables(seq):
    half = RET_DK // 2
    freqs = ROPE_THETA ** (-np.arange(half, dtype=np.float64) / half)
    ang = np.arange(seq, dtype=np.float64)[:, None] * freqs[None, :]
    log_g = np.log1p(-(2.0 ** (-5.0 - np.arange(RET_HEADS, dtype=np.float64))))
    idx = np.arange(CHUNK, dtype=np.float64)
    diff = idx[:, None] - idx[None, :]
    inner = np.where(diff >= 0, np.exp(np.maximum(diff, 0.0)[None] * log_g[:, None, None]), 0.0)
    cross = np.exp((idx[:, None] + 1.0) * log_g[None, :])
    state = np.exp((CHUNK - 1.0 - idx[:, None]) * log_g[None, :])
    chunk = np.exp(CHUNK * log_g)[None, :]
    per_head = lambda m: np.broadcast_to(m.T[:, :, None], (RET_HEADS, CHUNK, LANES))
    k_scale = RET_DK ** -0.5
    assert k_scale == 2.0 ** -4
    tables = (np.cos(ang), np.sin(ang), inner * k_scale, per_head(cross),
              per_head(state * k_scale), np.repeat(chunk, RET_DV, axis=1))
    return tuple(jnp.asarray(np.ascontiguousarray(t), dtype=jnp.float32) for t in tables)


def _const_spec(shape):
    zeros = (0,) * len(shape)
    return pl.BlockSpec(shape, lambda b, t: zeros, pipeline_mode=pl.Buffered(1))


def _layer(x, nin, win, cw, cb, gxw, gxb, gaw, gab, lam, gn, wpa, wpb, wout, nfin, tables,
           final_norm):
    B, S, D = x.shape
    assert D == D_MODEL and S % STEP_T == 0
    assert win.shape == (D, N_KINDS * D) and wpa.shape == wpb.shape == wout.shape == (D, D)
    bf16 = jnp.bfloat16
    row = lambda v: v.reshape(1, -1)
    gw = _pack_rows(0.5 * jnp.concatenate([_block_diag_groups(gxw), _block_diag_groups(gaw)], axis=-1))
    cos, sin, idec, cdec, sdec, chdec = tables
    operands = (
        x, row(nin), win, cw, row(cb), gw, row(gxb), row(gab), row(lam), row(gn),
        wpa, wpb, wout, row(nfin), cos, sin, idec, cdec, sdec, chdec)
    hbm_operands = (2, 10, 11, 12)
    tile_spec = pl.BlockSpec((1, STEP_T, D), lambda b, t: (b, t, 0))
    in_specs = []
    for i, o in enumerate(operands):
        if i == 0:
            in_specs.append(tile_spec)
        elif i in hbm_operands:
            in_specs.append(pl.BlockSpec(memory_space=pl.ANY))
        else:
            in_specs.append(_const_spec(o.shape))
    scratch = [
        pltpu.VMEM((D, N_KINDS * D), bf16),
        pltpu.VMEM((D, D), bf16),
        pltpu.VMEM((D, D), bf16),
        pltpu.VMEM((D, D), bf16),
        pltpu.VMEM((STAGE_SLOTS, D, STAGE_COLS), jnp.float32),
        pltpu.SemaphoreType.DMA((STAGE_SLOTS,)),
        pltpu.VMEM((TILE_T, D), bf16),
        pltpu.VMEM((D // LANES, TILE_T + HALO, LANES), jnp.float32),
        pltpu.VMEM((TILE_T, GROUP), jnp.float32),
        pltpu.VMEM((TILE_T, D), bf16),
        pltpu.VMEM((TILE_T, D), bf16),
        pltpu.VMEM((RET_HEADS, RET_DK, RET_DV), jnp.float32),
        pltpu.VMEM((SUBLANES, D), jnp.float32),
    ]
    return pl.pallas_call(
        functools.partial(_block_kernel, final_norm=final_norm),
        grid=(B, S // STEP_T),
        in_specs=in_specs,
        out_specs=tile_spec,
        out_shape=jax.ShapeDtypeStruct(x.shape, x.dtype),
        scratch_shapes=scratch,
        compiler_params=pltpu.CompilerParams(
            dimension_semantics=("arbitrary", "arbitrary"),
            vmem_limit_bytes=VMEM_LIMIT_BYTES),
        name="hybrid_block",
    )(*operands)


def kernel(x, norm_in, w_in, conv_w, conv_b, gate_x_w, gate_x_b, gate_a_w, gate_a_b, lru_lambda, gn_gain, w_proj_a, w_proj_b, w_out, norm_final):
    depth = w_in.shape[0]
    tables = _retention_tables(x.shape[1])
    for l in range(depth):
        x = _layer(x, norm_in[l], w_in[l], conv_w[l], conv_b[l], gate_x_w[l], gate_x_b[l],
                   gate_a_w[l], gate_a_b[l], lru_lambda[l], gn_gain[l].reshape(-1),
                   w_proj_a[l], w_proj_b[l], w_out[l], norm_final, tables,
                   final_norm=(l == depth - 1))
    return x
```

```python
import functools

import jax
import jax.numpy as jnp
import numpy as np
from jax import lax
from jax.experimental import pallas as pl
from jax.experimental.pallas import tpu as pltpu

D_MODEL = 1024
LRU_BLOCKS = 16
LRU_BW = D_MODEL // LRU_BLOCKS
CONV_WIDTH = 4
LRU_C = 8.0
RET_HEADS = 4
RET_DK = 256
RET_DV = 256
CHUNK = 256
ROPE_THETA = 10000.0
EPS = 1e-6

KIND_XA, KIND_GA, KIND_Q, KIND_K, KIND_V, KIND_GB, KIND_MA, KIND_MB = range(8)
N_KINDS = 8
N_GROUP_KINDS = 6
HALVED_KINDS = (KIND_GA, KIND_GB, KIND_MA, KIND_MB)

LANES = 128
SUBLANES = 8
GROUP = 256
N_GROUPS = D_MODEL // GROUP
GROUP_COLS = N_GROUP_KINDS * GROUP
MERGE_COL0 = N_GROUPS * GROUP_COLS
SEQS = 2
SEQ_T = 256
TILE_T = SEQS * SEQ_T
N_CHUNKS = SEQ_T // CHUNK
ROW_CHUNK = 32
STAGE_COLS = 256
STAGE_SLOTS = 4
PREP_ROWS = 64
HALO = SUBLANES
VMEM_LIMIT_BYTES = 56 * 1024 * 1024
TINY = 1e-37


def _dot(a, b):
    return jnp.dot(a, b, preferred_element_type=jnp.float32)


def _prepare_weights(win_hbm, wpa_hbm, wpb_hbm, wout_hbm, win_s, wpa_s, wpb_s, wout_s,
                     stage, sem):
    slabs = []
    for kind in range(N_KINDS):
        for c0 in range(0, D_MODEL, STAGE_COLS):
            src = win_hbm.at[:, pl.ds(kind * D_MODEL + c0, STAGE_COLS)]
            if kind < N_GROUP_KINDS:
                pieces = [(c - c0, (c // GROUP) * GROUP_COLS + kind * GROUP, GROUP)
                          for c in range(c0, c0 + STAGE_COLS, GROUP)]
            else:
                pieces = [(0, kind * D_MODEL + c0, STAGE_COLS)]
            slabs.append((src, win_s, pieces, kind in HALVED_KINDS))
    for src_hbm, dst, halve in ((wpa_hbm, wpa_s, True), (wpb_hbm, wpb_s, True),
                                (wout_hbm, wout_s, False)):
        for c0 in range(0, D_MODEL, STAGE_COLS):
            slabs.append((src_hbm.at[:, pl.ds(c0, STAGE_COLS)], dst,
                          [(0, c0, STAGE_COLS)], halve))

    def copy(n):
        slot = n % STAGE_SLOTS
        return pltpu.make_async_copy(slabs[n][0], stage.at[slot], sem.at[slot])

    for n in range(STAGE_SLOTS - 1):
        copy(n).start()
    for n, (_, dst, pieces, halve) in enumerate(slabs):
        if n + STAGE_SLOTS - 1 < len(slabs):
            copy(n + STAGE_SLOTS - 1).start()
        copy(n).wait()

        def convert(i, carry, slot=n % STAGE_SLOTS, dst=dst, pieces=pieces, halve=halve):
            rows = pl.ds(pl.multiple_of(i * PREP_ROWS, PREP_ROWS), PREP_ROWS)
            w = stage[slot, rows, :]
            if halve:
                w = 0.5 * w
            w = w.astype(jnp.bfloat16)
            for s0, d0, width in pieces:
                dst[rows, d0:d0 + width] = w[:, s0:s0 + width]
            return carry

        lax.fori_loop(0, D_MODEL // PREP_ROWS, convert, 0)


def _block_kernel(x_ref, nin_ref, win_hbm, cw_ref, cb_ref, gw_ref, gxb_ref, gab_ref, lam_ref,
                  gn_ref, wpa_hbm, wpb_hbm, wout_hbm, nfin_ref, cos_ref, sin_ref, idec_ref,
                  cdec_ref, sdec_ref, chdec_ref, out_ref,
                  win_s, wpa_s, wpb_s, wout_s, stage, sem,
                  h_ref, xa_ref, lru_ref, ya_ref, yb_ref, r_ref, hst_ref,
                  *, final_norm):
    @pl.when((pl.program_id(0) == 0) & (pl.program_id(1) == 0))
    def _():
        _prepare_weights(win_hbm, wpa_hbm, wpb_hbm, wout_hbm, win_s, wpa_s, wpb_s, wout_s,
                         stage, sem)

    _tile_body(pl.program_id(1),
               x_ref, nin_ref, cw_ref, cb_ref, gw_ref, gxb_ref, gab_ref, lam_ref, gn_ref,
               nfin_ref, cos_ref, sin_ref, idec_ref, cdec_ref, sdec_ref, chdec_ref, out_ref,
               win_s, wpa_s, wpb_s, wout_s, h_ref, xa_ref, lru_ref, ya_ref, yb_ref, r_ref,
               hst_ref, final_norm)


def _tile_body(t, x_ref, nin_ref, cw_ref, cb_ref, gw_ref, gxb_ref, gab_ref, lam_ref, gn_ref,
               nfin_ref, cos_ref, sin_ref, idec_ref, cdec_ref, sdec_ref, chdec_ref, out_ref,
               win_s, wpa_s, wpb_s, wout_s, h_ref, xa_ref, lru_ref, ya_ref, yb_ref, r_ref,
               hst_ref, final_norm):
    T = TILE_T
    bf16 = jnp.bfloat16
    seq_rows = [slice(s * SEQ_T, (s + 1) * SEQ_T) for s in range(SEQS)]
    stack = lambda parts: jnp.concatenate(parts, axis=0)

    @pl.when(t == 0)
    def _():
        r_ref[...] = jnp.zeros_like(r_ref)
        hst_ref[...] = jnp.zeros_like(hst_ref)
        xa_ref[:, :, 0:HALO, :] = jnp.zeros((SEQS, D_MODEL // LANES, HALO, LANES), jnp.float32)

    g_in = nin_ref[...]
    for r0 in range(0, T, ROW_CHUNK):
        xr = x_ref[r0 // SEQ_T, r0 % SEQ_T:r0 % SEQ_T + ROW_CHUNK, :]
        ms = jnp.mean(xr * xr, axis=-1, keepdims=True)
        h_ref[r0:r0 + ROW_CHUNK, :] = (xr * lax.rsqrt(ms + EPS) * g_in).astype(bf16)

    n_rg = T // SUBLANES
    sub = lax.broadcasted_iota(jnp.int32, (n_rg, SUBLANES, GROUP), 1)
    half = RET_DK // 2
    both_halves = lambda m: jnp.concatenate([m, m], axis=1)

    def project(g):
        return _dot(h_ref[...], win_s[:, g * GROUP_COLS:(g + 1) * GROUP_COLS])

    proj = project(0)
    for g in range(N_GROUPS):
        cs = slice(g * GROUP, (g + 1) * GROUP)
        kind = lambda i, p=proj: p[:, i * GROUP:(i + 1) * GROUP]

        xa = kind(KIND_XA)
        xc_blocks = []
        for b in range(GROUP // LANES):
            blk = g * (GROUP // LANES) + b
            lc = slice(blk * LANES, (blk + 1) * LANES)
            xa_b = xa[:, b * LANES:(b + 1) * LANES]
            for s in range(SEQS):
                xa_ref[s, blk, HALO:HALO + SEQ_T, :] = xa_b[seq_rows[s]]
            xc_b = cb_ref[:, lc] + cw_ref[CONV_WIDTH - 1:CONV_WIDTH, lc] * xa_b
            for k in range(CONV_WIDTH - 1):
                lo = HALO - (CONV_WIDTH - 1) + k
                taps = stack([xa_ref[s, blk, lo:lo + SEQ_T, :] for s in range(SEQS)])
                xc_b = xc_b + cw_ref[k:k + 1, lc] * taps
            for s in range(SEQS):
                xa_ref[s, blk, 0:HALO, :] = xa_ref[s, blk, SEQ_T:SEQ_T + HALO, :]
            xc_blocks.append(xc_b)
        xc = jnp.concatenate(xc_blocks, axis=1)
        gates = _dot(xc.astype(bf16), pltpu.bitcast(gw_ref[g], bf16))

        q, k, v, gb = kind(KIND_Q), kind(KIND_K), kind(KIND_V), kind(KIND_GB)
        ga = kind(KIND_GA)
        pos = pl.ds(pl.multiple_of(t * SEQ_T, SEQ_T), SEQ_T)
        cos = stack([cos_ref[pos, :]] * SEQS)
        sin = stack([sin_ref[pos, :]] * SEQS)
        q1, q2 = q[:, :half], q[:, half:]
        k1, k2 = k[:, :half], k[:, half:]
        qr = jnp.concatenate([q1 * cos - q2 * sin, q1 * sin + q2 * cos], axis=1).astype(bf16)
        kr = jnp.concatenate([k1 * cos - k2 * sin, k1 * sin + k2 * cos], axis=1).astype(bf16)
        v_bf = v.astype(bf16)
        chunk_rows = [slice(r0, r0 + CHUNK) for r0 in range(0, T, CHUNK)]
        scores, kv = [], []
        for rows in chunk_rows:
            scores.append(lax.dot_general(qr[rows], kr[rows], (((1,), (1,)), ((), ())),
                                          preferred_element_type=jnp.float32))
            kv.append(lax.dot_general(kr[rows], (v[rows] * both_halves(sdec_ref[g])).astype(bf16),
                                      (((0,), (0,)), ((), ())),
                                      preferred_element_type=jnp.float32))

        if g + 1 < N_GROUPS:
            proj = project(g + 1)
        else:
            mg = _dot(h_ref[...], win_s[:, MERGE_COL0:MERGE_COL0 + 2 * D_MODEL])

        ti1 = 1.0 + jnp.tanh(gates[:, :GROUP] + 0.5 * gxb_ref[:, cs])
        tr1 = 1.0 + jnp.tanh(gates[:, GROUP:] + 0.5 * gab_ref[:, cs])
        nlam = -lam_ref[:, cs]
        softplus = jnp.maximum(nlam, 0.0) + jnp.log1p(jnp.exp(-jnp.abs(nlam)))
        nla = tr1 * ((0.5 * LRU_C) * softplus)
        a = jnp.exp(-nla)
        x4 = jnp.tanh(nla) * (a * (0.25 * a) + 0.25)
        mult = x4 * lax.rsqrt(jnp.maximum(x4, TINY))
        u = mult * (ti1 * xc)

        a3 = a.reshape(n_rg, SUBLANES, GROUP)
        u3 = u.reshape(n_rg, SUBLANES, GROUP)
        for s in (1, 2, 4):
            valid = sub >= s
            u_sh = jnp.where(valid, pltpu.roll(u3, s, 1), 0.0)
            a_sh = jnp.where(valid, pltpu.roll(a3, s, 1), 1.0)
            u3 = u3 + a3 * u_sh
            a3 = a3 * a_sh
        for s in range(SEQS):
            carry = hst_ref[s, :, cs]
            for j in range(s * SEQ_T // SUBLANES, (s + 1) * SEQ_T // SUBLANES):
                hj = u3[j] + a3[j] * carry
                lru_ref[j * SUBLANES:(j + 1) * SUBLANES, :] = hj
                carry = jnp.broadcast_to(hj[SUBLANES - 1:SUBLANES, :], (SUBLANES, GROUP))
            hst_ref[s, :, cs] = carry
        ya_ref[:, cs] = (ga * (1.0 + jnp.tanh(ga)) * lru_ref[...]).astype(bf16)

        silu_gb = gb * (1.0 + jnp.tanh(gb))
        gain = gn_ref[:, cs]
        for s in range(SEQS):
            r_state = r_ref[s, g]
            for c in range(s * N_CHUNKS, (s + 1) * N_CHUNKS):
                rows = chunk_rows[c]
                cross = _dot(qr[rows], r_state.astype(bf16)) * both_halves(cdec_ref[g])
                r_state = chdec_ref[:, cs] * r_state + kv[c]
                inner = _dot((scores[c] * idec_ref[g]).astype(bf16), v_bf[rows])
                y = inner + cross
                mu = jnp.mean(y, axis=-1, keepdims=True)
                yc = y - mu
                var = jnp.mean(yc * yc, axis=-1, keepdims=True)
                gn = yc * lax.rsqrt(var + EPS) * gain
                yb_ref[rows, cs] = (silu_gb[rows] * gn).astype(bf16)
            r_ref[s, g] = r_state

    oa = _dot(ya_ref[...], wpa_s[...])
    ob = _dot(yb_ref[...], wpb_s[...])
    merged = ((1.0 + jnp.tanh(mg[:, :D_MODEL])) * oa
              + (1.0 + jnp.tanh(mg[:, D_MODEL:])) * ob).astype(bf16)
    delta = _dot(merged, wout_s[...])
    for s in range(SEQS):
        out_ref[s] = delta[seq_rows[s]]
    g_fin = nfin_ref[...]
    for r0 in range(0, T, ROW_CHUNK):
        s, rows = r0 // SEQ_T, slice(r0 % SEQ_T, r0 % SEQ_T + ROW_CHUNK)
        xo = x_ref[s, rows, :] + out_ref[s, rows, :]
        if final_norm:
            ms = jnp.mean(xo * xo, axis=-1, keepdims=True)
            xo = xo * lax.rsqrt(ms + EPS) * g_fin
        out_ref[s, rows, :] = xo


def _block_diag_groups(w):
    per = GROUP // LRU_BW
    w4 = w.reshape(N_GROUPS, per, LRU_BW, LRU_BW)
    eye = jnp.eye(per, dtype=w.dtype)
    return jnp.einsum('gikn,ij->gikjn', w4, eye).reshape(N_GROUPS, GROUP, GROUP)


def _pack_rows(w):
    *lead, k, n = w.shape
    pairs = w.astype(jnp.bfloat16).reshape(*lead, k // 2, 2, n)
    return lax.bitcast_convert_type(jnp.swapaxes(pairs, -1, -2), jnp.uint32)


def _retention_tables(seq):
    half = RET_DK // 2
    freqs = ROPE_THETA ** (-np.arange(half, dtype=np.float64) / half)
    ang = np.arange(seq, dtype=np.float64)[:, None] * freqs[None, :]
    log_g = np.log1p(-(2.0 ** (-5.0 - np.arange(RET_HEADS, dtype=np.float64))))
    idx = np.arange(CHUNK, dtype=np.float64)
    diff = idx[:, None] - idx[None, :]
    inner = np.where(diff >= 0, np.exp(np.maximum(diff, 0.0)[None] * log_g[:, None, None]), 0.0)
    cross = np.exp((idx[:, None] + 1.0) * log_g[None, :])
    state = np.exp((CHUNK - 1.0 - idx[:, None]) * log_g[None, :])
    chunk = np.exp(CHUNK * log_g)[None, :]
    per_head = lambda m: np.broadcast_to(m.T[:, :, None], (RET_HEADS, CHUNK, LANES))
    k_scale = RET_DK ** -0.5
    assert k_scale == 2.0 ** -4
    tables = (np.cos(ang), np.sin(ang), inner * k_scale, per_head(cross),
              per_head(state * k_scale), np.repeat(chunk, RET_DV, axis=1))
    return tuple(jnp.asarray(np.ascontiguousarray(t), dtype=jnp.float32) for t in tables)


def _const_spec(shape):
    zeros = (0,) * len(shape)
    return pl.BlockSpec(shape, lambda b, t: zeros, pipeline_mode=pl.Buffered(1))


def _layer(x, nin, win, cw, cb, gxw, gxb, gaw, gab, lam, gn, wpa, wpb, wout, nfin, tables,
           final_norm):
    B, S, D = x.shape
    assert D == D_MODEL and S % SEQ_T == 0 and B % SEQS == 0
    assert win.shape == (D, N_KINDS * D) and wpa.shape == wpb.shape == wout.shape == (D, D)
    bf16 = jnp.bfloat16
    row = lambda v: v.reshape(1, -1)
    gw = _pack_rows(0.5 * jnp.concatenate([_block_diag_groups(gxw), _block_diag_groups(gaw)], axis=-1))
    cos, sin, idec, cdec, sdec, chdec = tables
    operands = (
        x, row(nin), win, cw, row(cb), gw, row(gxb), row(gab), row(lam), row(gn),
        wpa, wpb, wout, row(nfin), cos, sin, idec, cdec, sdec, chdec)
    hbm_operands = (2, 10, 11, 12)
    tile_spec = pl.BlockSpec((SEQS, SEQ_T, D), lambda b, t: (b, t, 0))
    in_specs = []
    for i, o in enumerate(operands):
        if i == 0:
            in_specs.append(tile_spec)
        elif i in hbm_operands:
            in_specs.append(pl.BlockSpec(memory_space=pl.ANY))
        else:
            in_specs.append(_const_spec(o.shape))
    scratch = [
        pltpu.VMEM((D, N_KINDS * D), bf16),
        pltpu.VMEM((D, D), bf16),
        pltpu.VMEM((D, D), bf16),
        pltpu.VMEM((D, D), bf16),
        pltpu.VMEM((STAGE_SLOTS, D, STAGE_COLS), jnp.float32),
        pltpu.SemaphoreType.DMA((STAGE_SLOTS,)),
        pltpu.VMEM((TILE_T, D), bf16),
        pltpu.VMEM((SEQS, D // LANES, SEQ_T + HALO, LANES), jnp.float32),
        pltpu.VMEM((TILE_T, GROUP), jnp.float32),
        pltpu.VMEM((TILE_T, D), bf16),
        pltpu.VMEM((TILE_T, D), bf16),
        pltpu.VMEM((SEQS, RET_HEADS, RET_DK, RET_DV), jnp.float32),
        pltpu.VMEM((SEQS, SUBLANES, D), jnp.float32),
    ]
    return pl.pallas_call(
        functools.partial(_block_kernel, final_norm=final_norm),
        grid=(B // SEQS, S // SEQ_T),
        in_specs=in_specs,
        out_specs=tile_spec,
        out_shape=jax.ShapeDtypeStruct(x.shape, x.dtype),
        scratch_shapes=scratch,
        compiler_params=pltpu.CompilerParams(
            dimension_semantics=("arbitrary", "arbitrary"),
            vmem_limit_bytes=VMEM_LIMIT_BYTES),
        name="hybrid_block",
    )(*operands)


def kernel(x, norm_in, w_in, conv_w, conv_b, gate_x_w, gate_x_b, gate_a_w, gate_a_b, lru_lambda, gn_gain, w_proj_a, w_proj_b, w_out, norm_final):
    depth = w_in.shape[0]
    tables = _retention_tables(x.shape[1])
    for l in range(depth):
        x = _layer(x, norm_in[l], w_in[l], conv_w[l], conv_b[l], gate_x_w[l], gate_x_b[l],
                   gate_a_w[l], gate_a_b[l], lru_lambda[l], gn_gain[l].reshape(-1),
                   w_proj_a[l], w_proj_b[l], w_out[l], norm_final, tables,
                   final_norm=(l == depth - 1))
    return x
```

```python
import functools

import jax
import jax.numpy as jnp
import numpy as np
from jax import lax
from jax.experimental import pallas as pl
from jax.experimental.pallas import tpu as pltpu

D_MODEL = 1024
LRU_BLOCKS = 16
LRU_BW = D_MODEL // LRU_BLOCKS
CONV_WIDTH = 4
LRU_C = 8.0
RET_HEADS = 4
RET_DK = 256
RET_DV = 256
CHUNK = 256
ROPE_THETA = 10000.0
EPS = 1e-6

KIND_XA, KIND_GA, KIND_Q, KIND_K, KIND_V, KIND_GB, KIND_MA, KIND_MB = range(8)
N_KINDS = 8
N_GROUP_KINDS = 6
HALVED_KINDS = (KIND_GA, KIND_GB, KIND_MA, KIND_MB)

LANES = 128
SUBLANES = 8
GROUP = 256
N_GROUPS = D_MODEL // GROUP
GROUP_COLS = N_GROUP_KINDS * GROUP
MERGE_COL0 = N_GROUPS * GROUP_COLS
TILE_T = 512
TILES_PER_STEP = 1
STEP_T = TILE_T * TILES_PER_STEP
N_CHUNKS = TILE_T // CHUNK
ROW_CHUNK = 32
STAGE_COLS = 256
STAGE_SLOTS = 4
PREP_ROWS = 64
HALO = SUBLANES
VMEM_LIMIT_BYTES = 56 * 1024 * 1024
TINY = 1e-37


def _dot(a, b):
    return jnp.dot(a, b, preferred_element_type=jnp.float32)


def _prepare_weights(win_hbm, wpa_hbm, wpb_hbm, wout_hbm, win_s, wpa_s, wpb_s, wout_s,
                     stage, sem):
    slabs = []
    for kind in range(N_KINDS):
        for c0 in range(0, D_MODEL, STAGE_COLS):
            src = win_hbm.at[:, pl.ds(kind * D_MODEL + c0, STAGE_COLS)]
            if kind < N_GROUP_KINDS:
                pieces = [(c - c0, (c // GROUP) * GROUP_COLS + kind * GROUP, GROUP)
                          for c in range(c0, c0 + STAGE_COLS, GROUP)]
            else:
                pieces = [(0, kind * D_MODEL + c0, STAGE_COLS)]
            slabs.append((src, win_s, pieces, kind in HALVED_KINDS))
    for src_hbm, dst, halve in ((wpa_hbm, wpa_s, True), (wpb_hbm, wpb_s, True),
                                (wout_hbm, wout_s, False)):
        for c0 in range(0, D_MODEL, STAGE_COLS):
            slabs.append((src_hbm.at[:, pl.ds(c0, STAGE_COLS)], dst,
                          [(0, c0, STAGE_COLS)], halve))

    def copy(n):
        slot = n % STAGE_SLOTS
        return pltpu.make_async_copy(slabs[n][0], stage.at[slot], sem.at[slot])

    for n in range(STAGE_SLOTS - 1):
        copy(n).start()
    for n, (_, dst, pieces, halve) in enumerate(slabs):
        if n + STAGE_SLOTS - 1 < len(slabs):
            copy(n + STAGE_SLOTS - 1).start()
        copy(n).wait()

        def convert(i, carry, slot=n % STAGE_SLOTS, dst=dst, pieces=pieces, halve=halve):
            rows = pl.ds(pl.multiple_of(i * PREP_ROWS, PREP_ROWS), PREP_ROWS)
            w = stage[slot, rows, :]
            if halve:
                w = 0.5 * w
            w = w.astype(jnp.bfloat16)
            for s0, d0, width in pieces:
                dst[rows, d0:d0 + width] = w[:, s0:s0 + width]
            return carry

        lax.fori_loop(0, D_MODEL // PREP_ROWS, convert, 0)


def _block_kernel(x_ref, nin_ref, win_hbm, cw_ref, cb_ref, gw_ref, gxb_ref, gab_ref, lam_ref,
                  gn_ref, wpa_hbm, wpb_hbm, wout_hbm, nfin_ref, cos_ref, sin_ref, idec_ref,
                  cdec_ref, sdec_ref, chdec_ref, out_ref,
                  win_s, wpa_s, wpb_s, wout_s, stage, sem,
                  h_ref, xa_ref, lru_ref, ya_ref, yb_ref, r_ref, hst_ref,
                  *, final_norm):
    @pl.when((pl.program_id(0) == 0) & (pl.program_id(1) == 0))
    def _():
        _prepare_weights(win_hbm, wpa_hbm, wpb_hbm, wout_hbm, win_s, wpa_s, wpb_s, wout_s,
                         stage, sem)

    def tile(i, row0):
        _tile_body(pl.program_id(1) * TILES_PER_STEP + i, row0,
                   x_ref, nin_ref, cw_ref, cb_ref, gw_ref, gxb_ref, gab_ref, lam_ref, gn_ref,
                   nfin_ref, cos_ref, sin_ref, idec_ref, cdec_ref, sdec_ref, chdec_ref, out_ref,
                   win_s, wpa_s, wpb_s, wout_s, h_ref, xa_ref, lru_ref, ya_ref, yb_ref, r_ref,
                   hst_ref, final_norm)

    for i in range(TILES_PER_STEP):
        tile(i, i * TILE_T)


def _tile_body(t, row0, x_ref, nin_ref, cw_ref, cb_ref, gw_ref, gxb_ref, gab_ref, lam_ref, gn_ref,
               nfin_ref, cos_ref, sin_ref, idec_ref, cdec_ref, sdec_ref, chdec_ref, out_ref,
               win_s, wpa_s, wpb_s, wout_s, h_ref, xa_ref, lru_ref, ya_ref, yb_ref, r_ref,
               hst_ref, final_norm):
    T = TILE_T
    bf16 = jnp.bfloat16

    @pl.when(t == 0)
    def _():
        r_ref[...] = jnp.zeros_like(r_ref)
        hst_ref[...] = jnp.zeros_like(hst_ref)
        xa_ref[:, 0:HALO, :] = jnp.zeros((D_MODEL // LANES, HALO, LANES), jnp.float32)

    g_in = nin_ref[...]
    for r0 in range(0, T, ROW_CHUNK):
        xr = x_ref[0, pl.ds(row0 + r0, ROW_CHUNK), :]
        ms = jnp.mean(xr * xr, axis=-1, keepdims=True)
        h_ref[r0:r0 + ROW_CHUNK, :] = (xr * lax.rsqrt(ms + EPS) * g_in).astype(bf16)

    n_rg = T // SUBLANES
    sub = lax.broadcasted_iota(jnp.int32, (n_rg, SUBLANES, GROUP), 1)
    half = RET_DK // 2
    both_halves = lambda m: jnp.concatenate([m, m], axis=1)

    def project(g):
        return _dot(h_ref[...], win_s[:, g * GROUP_COLS:(g + 1) * GROUP_COLS])

    proj = project(0)
    for g in range(N_GROUPS):
        cs = slice(g * GROUP, (g + 1) * GROUP)
        kind = lambda i, p=proj: p[:, i * GROUP:(i + 1) * GROUP]
        if g + 1 == N_GROUPS:
            done = g * GROUP
            oa_head = _dot(ya_ref[:, :done], wpa_s[:done, :])
            ob_head = _dot(yb_ref[:, :done], wpb_s[:done, :])

        xa = kind(KIND_XA)
        xc_blocks = []
        for b in range(GROUP // LANES):
            blk = g * (GROUP // LANES) + b
            lc = slice(blk * LANES, (blk + 1) * LANES)
            xa_b = xa[:, b * LANES:(b + 1) * LANES]
            xa_ref[blk, HALO:HALO + T, :] = xa_b
            xc_b = cb_ref[:, lc] + cw_ref[CONV_WIDTH - 1:CONV_WIDTH, lc] * xa_b
            for k in range(CONV_WIDTH - 1):
                lo = HALO - (CONV_WIDTH - 1) + k
                xc_b = xc_b + cw_ref[k:k + 1, lc] * xa_ref[blk, lo:lo + T, :]
            xa_ref[blk, 0:HALO, :] = xa_ref[blk, T:T + HALO, :]
            xc_blocks.append(xc_b)
        xc = jnp.concatenate(xc_blocks, axis=1)
        gates = _dot(xc.astype(bf16), pltpu.bitcast(gw_ref[g], bf16))

        q, k, v, gb = kind(KIND_Q), kind(KIND_K), kind(KIND_V), kind(KIND_GB)
        ga = kind(KIND_GA)
        pos = pl.ds(pl.multiple_of(t * T, T), T)
        cos = cos_ref[pos, :]
        sin = sin_ref[pos, :]
        q1, q2 = q[:, :half], q[:, half:]
        k1, k2 = k[:, :half], k[:, half:]
        qr = jnp.concatenate([q1 * cos - q2 * sin, q1 * sin + q2 * cos], axis=1).astype(bf16)
        kr = jnp.concatenate([k1 * cos - k2 * sin, k1 * sin + k2 * cos], axis=1).astype(bf16)
        v_bf = v.astype(bf16)
        scores, kv = [], []
        for c in range(N_CHUNKS):
            rows = slice(c * CHUNK, (c + 1) * CHUNK)
            scores.append(lax.dot_general(qr[rows], kr[rows], (((1,), (1,)), ((), ())),
                                          preferred_element_type=jnp.float32))
            kv.append(lax.dot_general(kr[rows], (v[rows] * both_halves(sdec_ref[g])).astype(bf16),
                                      (((0,), (0,)), ((), ())),
                                      preferred_element_type=jnp.float32))

        if g + 1 < N_GROUPS:
            proj = project(g + 1)
        else:
            mg = _dot(h_ref[...], win_s[:, MERGE_COL0:MERGE_COL0 + 2 * D_MODEL])

        ti1 = 1.0 + jnp.tanh(gates[:, :GROUP] + 0.5 * gxb_ref[:, cs])
        tr1 = 1.0 + jnp.tanh(gates[:, GROUP:] + 0.5 * gab_ref[:, cs])
        nlam = -lam_ref[:, cs]
        softplus = jnp.maximum(nlam, 0.0) + jnp.log1p(jnp.exp(-jnp.abs(nlam)))
        nla = tr1 * ((0.5 * LRU_C) * softplus)
        a = jnp.exp(-nla)
        x4 = jnp.tanh(nla) * (a * (0.25 * a) + 0.25)
        mult = x4 * lax.rsqrt(jnp.maximum(x4, TINY))
        u = mult * (ti1 * xc)

        a3 = a.reshape(n_rg, SUBLANES, GROUP)
        u3 = u.reshape(n_rg, SUBLANES, GROUP)
        for s in (1, 2, 4):
            valid = sub >= s
            u_sh = jnp.where(valid, pltpu.roll(u3, s, 1), 0.0)
            a_sh = jnp.where(valid, pltpu.roll(a3, s, 1), 1.0)
            u3 = u3 + a3 * u_sh
            a3 = a3 * a_sh
        carry = hst_ref[:, cs]
        for j in range(n_rg):
            hj = u3[j] + a3[j] * carry
            lru_ref[j * SUBLANES:(j + 1) * SUBLANES, :] = hj
            carry = jnp.broadcast_to(hj[SUBLANES - 1:SUBLANES, :], (SUBLANES, GROUP))
        hst_ref[:, cs] = carry
        ya_ref[:, cs] = (ga * (1.0 + jnp.tanh(ga)) * lru_ref[...]).astype(bf16)

        silu_gb = gb * (1.0 + jnp.tanh(gb))
        gain = gn_ref[:, cs]
        r_state = r_ref[g]
        for c in range(N_CHUNKS):
            rows = slice(c * CHUNK, (c + 1) * CHUNK)
            cross = _dot(qr[rows], r_state.astype(bf16)) * both_halves(cdec_ref[g])
            r_state = chdec_ref[:, cs] * r_state + kv[c]
            inner = _dot((scores[c] * idec_ref[g]).astype(bf16), v_bf[rows])
            y = inner + cross
            mu = jnp.mean(y, axis=-1, keepdims=True)
            yc = y - mu
            var = jnp.mean(yc * yc, axis=-1, keepdims=True)
            gn = yc * lax.rsqrt(var + EPS) * gain
            yb_ref[rows, cs] = (silu_gb[rows] * gn).astype(bf16)
        r_ref[g] = r_state

    oa = oa_head + _dot(ya_ref[:, done:], wpa_s[done:, :])
    ob = ob_head + _dot(yb_ref[:, done:], wpb_s[done:, :])
    merged = ((1.0 + jnp.tanh(mg[:, :D_MODEL])) * oa
              + (1.0 + jnp.tanh(mg[:, D_MODEL:])) * ob).astype(bf16)
    out_ref[0, pl.ds(row0, T), :] = _dot(merged, wout_s[...])
    g_fin = nfin_ref[...]
    for r0 in range(0, T, ROW_CHUNK):
        rows = pl.ds(row0 + r0, ROW_CHUNK)
        xo = x_ref[0, rows, :] + out_ref[0, rows, :]
        if final_norm:
            ms = jnp.mean(xo * xo, axis=-1, keepdims=True)
            xo = xo * lax.rsqrt(ms + EPS) * g_fin
        out_ref[0, rows, :] = xo


def _block_diag_groups(w):
    per = GROUP // LRU_BW
    w4 = w.reshape(N_GROUPS, per, LRU_BW, LRU_BW)
    eye = jnp.eye(per, dtype=w.dtype)
    return jnp.einsum('gikn,ij->gikjn', w4, eye).reshape(N_GROUPS, GROUP, GROUP)


def _pack_rows(w):
    *lead, k, n = w.shape
    pairs = w.astype(jnp.bfloat16).reshape(*lead, k // 2, 2, n)
    return lax.bitcast_convert_type(jnp.swapaxes(pairs, -1, -2), jnp.uint32)


def _retention_tables(seq):
    half = RET_DK // 2
    freqs = ROPE_THETA ** (-np.arange(half, dtype=np.float64) / half)
    ang = np.arange(seq, dtype=np.float64)[:, None] * freqs[None, :]
    log_g = np.log1p(-(2.0 ** (-5.0 - np.arange(RET_HEADS, dtype=np.float64))))
    idx = np.arange(CHUNK, dtype=np.float64)
    diff = idx[:, None] - idx[None, :]
    inner = np.where(diff >= 0, np.exp(np.maximum(diff, 0.0)[None] * log_g[:, None, None]), 0.0)
    cross = np.exp((idx[:, None] + 1.0) * log_g[None, :])
    state = np.exp((CHUNK - 1.0 - idx[:, None]) * log_g[None, :])
    chunk = np.exp(CHUNK * log_g)[None, :]
    per_head = lambda m: np.broadcast_to(m.T[:, :, None], (RET_HEADS, CHUNK, LANES))
    k_scale = RET_DK ** -0.5
    assert k_scale == 2.0 ** -4
    tables = (np.cos(ang), np.sin(ang), inner * k_scale, per_head(cross),
              per_head(state * k_scale), np.repeat(chunk, RET_DV, axis=1))
    return tuple(jnp.asarray(np.ascontiguousarray(t), dtype=jnp.float32) for t in tables)


def _const_spec(shape):
    zeros = (0,) * len(shape)
    return pl.BlockSpec(shape, lambda b, t: zeros, pipeline_mode=pl.Buffered(1))


def _layer(x, nin, win, cw, cb, gxw, gxb, gaw, gab, lam, gn, wpa, wpb, wout, nfin, tables,
           final_norm):
    B, S, D = x.shape
    assert D == D_MODEL and S % STEP_T == 0
    assert win.shape == (D, N_KINDS * D) and wpa.shape == wpb.shape == wout.shape == (D, D)
    bf16 = jnp.bfloat16
    row = lambda v: v.reshape(1, -1)
    gw = _pack_rows(0.5 * jnp.concatenate([_block_diag_groups(gxw), _block_diag_groups(gaw)], axis=-1))
    cos, sin, idec, cdec, sdec, chdec = tables
    operands = (
        x, row(nin), win, cw, row(cb), gw, row(gxb), row(gab), row(lam), row(gn),
        wpa, wpb, wout, row(nfin), cos, sin, idec, cdec, sdec, chdec)
    hbm_operands = (2, 10, 11, 12)
    tile_spec = pl.BlockSpec((1, STEP_T, D), lambda b, t: (b, t, 0))
    in_specs = []
    for i, o in enumerate(operands):
        if i == 0:
            in_specs.append(tile_spec)
        elif i in hbm_operands:
            in_specs.append(pl.BlockSpec(memory_space=pl.ANY))
        else:
            in_specs.append(_const_spec(o.shape))
    scratch = [
        pltpu.VMEM((D, N_KINDS * D), bf16),
        pltpu.VMEM((D, D), bf16),
        pltpu.VMEM((D, D), bf16),
        pltpu.VMEM((D, D), bf16),
        pltpu.VMEM((STAGE_SLOTS, D, STAGE_COLS), jnp.float32),
        pltpu.SemaphoreType.DMA((STAGE_SLOTS,)),
        pltpu.VMEM((TILE_T, D), bf16),
        pltpu.VMEM((D // LANES, TILE_T + HALO, LANES), jnp.float32),
        pltpu.VMEM((TILE_T, GROUP), jnp.float32),
        pltpu.VMEM((TILE_T, D), bf16),
        pltpu.VMEM((TILE_T, D), bf16),
        pltpu.VMEM((RET_HEADS, RET_DK, RET_DV), jnp.float32),
        pltpu.VMEM((SUBLANES, D), jnp.float32),
    ]
    return pl.pallas_call(
        functools.partial(_block_kernel, final_norm=final_norm),
        grid=(B, S // STEP_T),
        in_specs=in_specs,
        out_specs=tile_spec,
        out_shape=jax.ShapeDtypeStruct(x.shape, x.dtype),
        scratch_shapes=scratch,
        compiler_params=pltpu.CompilerParams(
            dimension_semantics=("arbitrary", "arbitrary"),
            vmem_limit_bytes=VMEM_LIMIT_BYTES),
        name="hybrid_block",
    )(*operands)


def kernel(x, norm_in, w_in, conv_w, conv_b, gate_x_w, gate_x_b, gate_a_w, gate_a_b, lru_lambda, gn_gain, w_proj_a, w_proj_b, w_out, norm_final):
    depth = w_in.shape[0]
    tables = _retention_tables(x.shape[1])
    for l in range(depth):
        x = _layer(x, norm_in[l], w_in[l], conv_w[l], conv_b[l], gate_x_w[l], gate_x_b[l],
                   gate_a_w[l], gate_a_b[l], lru_lambda[l], gn_gain[l].reshape(-1),
                   w_proj_a[l], w_proj_b[l], w_out[l], norm_final, tables,
                   final_norm=(l == depth - 1))
    return x
```

```python
import functools

import jax
import jax.numpy as jnp
import numpy as np
from jax import lax
from jax.experimental import pallas as pl
from jax.experimental.pallas import tpu as pltpu

D_MODEL = 1024
LRU_BLOCKS = 16
LRU_BW = D_MODEL // LRU_BLOCKS
CONV_WIDTH = 4
LRU_C = 8.0
RET_HEADS = 4
RET_DK = 256
RET_DV = 256
CHUNK = 256
ROPE_THETA = 10000.0
EPS = 1e-6

KIND_XA, KIND_GA, KIND_Q, KIND_K, KIND_V, KIND_GB, KIND_MA, KIND_MB = range(8)
N_KINDS = 8
N_GROUP_KINDS = 6
HALVED_KINDS = (KIND_GA, KIND_GB, KIND_MA, KIND_MB)

LANES = 128
SUBLANES = 8
GROUP = 256
N_GROUPS = D_MODEL // GROUP
GROUP_COLS = N_GROUP_KINDS * GROUP
MERGE_COL0 = N_GROUPS * GROUP_COLS
TILE_T = 512
TILES_PER_STEP = 1
STEP_T = TILE_T * TILES_PER_STEP
N_CHUNKS = TILE_T // CHUNK
ROW_CHUNK = 32
STAGE_COLS = 256
STAGE_SLOTS = 4
PREP_ROWS = 64
HALO = SUBLANES
VMEM_LIMIT_BYTES = 56 * 1024 * 1024
TINY = 1e-37


def _dot(a, b):
    return jnp.dot(a, b, preferred_element_type=jnp.float32)


def _prepare_weights(win_hbm, wpa_hbm, wpb_hbm, wout_hbm, win_s, wpa_s, wpb_s, wout_s,
                     stage, sem):
    slabs = []
    for kind in range(N_KINDS):
        for c0 in range(0, D_MODEL, STAGE_COLS):
            src = win_hbm.at[:, pl.ds(kind * D_MODEL + c0, STAGE_COLS)]
            if kind < N_GROUP_KINDS:
                pieces = [(c - c0, (c // GROUP) * GROUP_COLS + kind * GROUP, GROUP)
                          for c in range(c0, c0 + STAGE_COLS, GROUP)]
            else:
                pieces = [(0, kind * D_MODEL + c0, STAGE_COLS)]
            slabs.append((src, win_s, pieces, kind in HALVED_KINDS))
    for src_hbm, dst, halve in ((wpa_hbm, wpa_s, True), (wpb_hbm, wpb_s, True),
                                (wout_hbm, wout_s, False)):
        for c0 in range(0, D_MODEL, STAGE_COLS):
            slabs.append((src_hbm.at[:, pl.ds(c0, STAGE_COLS)], dst,
                          [(0, c0, STAGE_COLS)], halve))

    def copy(n):
        slot = n % STAGE_SLOTS
        return pltpu.make_async_copy(slabs[n][0], stage.at[slot], sem.at[slot])

    for n in range(STAGE_SLOTS - 1):
        copy(n).start()
    for n, (_, dst, pieces, halve) in enumerate(slabs):
        if n + STAGE_SLOTS - 1 < len(slabs):
            copy(n + STAGE_SLOTS - 1).start()
        copy(n).wait()

        def convert(i, carry, slot=n % STAGE_SLOTS, dst=dst, pieces=pieces, halve=halve):
            rows = pl.ds(pl.multiple_of(i * PREP_ROWS, PREP_ROWS), PREP_ROWS)
            w = stage[slot, rows, :]
            if halve:
                w = 0.5 * w
            w = w.astype(jnp.bfloat16)
            for s0, d0, width in pieces:
                dst[rows, d0:d0 + width] = w[:, s0:s0 + width]
            return carry

        lax.fori_loop(0, D_MODEL // PREP_ROWS, convert, 0)


def _block_kernel(x_ref, nin_ref, win_hbm, cw_ref, cb_ref, gw_ref, gxb_ref, gab_ref, lam_ref,
                  gn_ref, wpa_hbm, wpb_hbm, wout_hbm, nfin_ref, cos_ref, sin_ref, idec_ref,
                  cdec_ref, sdec_ref, chdec_ref, out_ref,
                  win_s, wpa_s, wpb_s, wout_s, gw_s, stage, sem,
                  h_ref, xa_ref, lru_ref, ya_ref, yb_ref, r_ref, hst_ref,
                  *, final_norm):
    @pl.when((pl.program_id(0) == 0) & (pl.program_id(1) == 0))
    def _():
        _prepare_weights(win_hbm, wpa_hbm, wpb_hbm, wout_hbm, win_s, wpa_s, wpb_s, wout_s,
                         stage, sem)
        for g in range(N_GROUPS):
            gw_s[g] = gw_ref[g]

    def tile(i, row0):
        _tile_body(pl.program_id(1) * TILES_PER_STEP + i, row0,
                   x_ref, nin_ref, cw_ref, cb_ref, gw_s, gxb_ref, gab_ref, lam_ref, gn_ref,
                   nfin_ref, cos_ref, sin_ref, idec_ref, cdec_ref, sdec_ref, chdec_ref, out_ref,
                   win_s, wpa_s, wpb_s, wout_s, h_ref, xa_ref, lru_ref, ya_ref, yb_ref, r_ref,
                   hst_ref, final_norm)

    for i in range(TILES_PER_STEP):
        tile(i, i * TILE_T)


def _tile_body(t, row0, x_ref, nin_ref, cw_ref, cb_ref, gw_ref, gxb_ref, gab_ref, lam_ref, gn_ref,
               nfin_ref, cos_ref, sin_ref, idec_ref, cdec_ref, sdec_ref, chdec_ref, out_ref,
               win_s, wpa_s, wpb_s, wout_s, h_ref, xa_ref, lru_ref, ya_ref, yb_ref, r_ref,
               hst_ref, final_norm):
    T = TILE_T
    bf16 = jnp.bfloat16

    @pl.when(t == 0)
    def _():
        r_ref[...] = jnp.zeros_like(r_ref)
        hst_ref[...] = jnp.zeros_like(hst_ref)
        xa_ref[:, 0:HALO, :] = jnp.zeros((D_MODEL // LANES, HALO, LANES), jnp.float32)

    g_in = nin_ref[...]
    for r0 in range(0, T, ROW_CHUNK):
        xr = x_ref[0, pl.ds(row0 + r0, ROW_CHUNK), :]
        ms = jnp.mean(xr * xr, axis=-1, keepdims=True)
        h_ref[r0:r0 + ROW_CHUNK, :] = (xr * lax.rsqrt(ms + EPS) * g_in).astype(bf16)

    n_rg = T // SUBLANES
    sub = lax.broadcasted_iota(jnp.int32, (n_rg, SUBLANES, GROUP), 1)
    half = RET_DK // 2
    both_halves = lambda m: jnp.concatenate([m, m], axis=1)

    def project(g):
        return _dot(h_ref[...], win_s[:, g * GROUP_COLS:(g + 1) * GROUP_COLS])

    proj = project(0)
    mg_parts = []
    for g in range(N_GROUPS):
        cs = slice(g * GROUP, (g + 1) * GROUP)
        kind = lambda i, p=proj: p[:, i * GROUP:(i + 1) * GROUP]
        if g + 1 == N_GROUPS:
            done = g * GROUP
            oa_head = _dot(ya_ref[:, :done], wpa_s[:done, :])
            ob_head = _dot(yb_ref[:, :done], wpb_s[:done, :])

        xa = kind(KIND_XA)
        xc_blocks = []
        for b in range(GROUP // LANES):
            blk = g * (GROUP // LANES) + b
            lc = slice(blk * LANES, (blk + 1) * LANES)
            xa_b = xa[:, b * LANES:(b + 1) * LANES]
            xa_ref[blk, HALO:HALO + T, :] = xa_b
            xc_b = cb_ref[:, lc] + cw_ref[CONV_WIDTH - 1:CONV_WIDTH, lc] * xa_b
            for k in range(CONV_WIDTH - 1):
                lo = HALO - (CONV_WIDTH - 1) + k
                xc_b = xc_b + cw_ref[k:k + 1, lc] * xa_ref[blk, lo:lo + T, :]
            xa_ref[blk, 0:HALO, :] = xa_ref[blk, T:T + HALO, :]
            xc_blocks.append(xc_b)
        xc = jnp.concatenate(xc_blocks, axis=1)
        gates = _dot(xc.astype(bf16), gw_ref[g])

        q, k, v, gb = kind(KIND_Q), kind(KIND_K), kind(KIND_V), kind(KIND_GB)
        ga = kind(KIND_GA)
        pos = pl.ds(pl.multiple_of(t * T, T), T)
        cos = cos_ref[pos, :]
        sin = sin_ref[pos, :]
        q1, q2 = q[:, :half], q[:, half:]
        k1, k2 = k[:, :half], k[:, half:]
        qr = jnp.concatenate([q1 * cos - q2 * sin, q1 * sin + q2 * cos], axis=1).astype(bf16)
        kr = jnp.concatenate([k1 * cos - k2 * sin, k1 * sin + k2 * cos], axis=1).astype(bf16)
        v_bf = v.astype(bf16)
        scores, kv = [], []
        for c in range(N_CHUNKS):
            rows = slice(c * CHUNK, (c + 1) * CHUNK)
            scores.append(lax.dot_general(qr[rows], kr[rows], (((1,), (1,)), ((), ())),
                                          preferred_element_type=jnp.float32))
            kv.append(lax.dot_general(kr[rows], (v[rows] * both_halves(sdec_ref[g])).astype(bf16),
                                      (((0,), (0,)), ((), ())),
                                      preferred_element_type=jnp.float32))

        if g + 1 < N_GROUPS:
            proj = project(g + 1)
        q0 = MERGE_COL0 + g * (2 * D_MODEL // N_GROUPS)
        mg_parts.append(_dot(h_ref[...], win_s[:, q0:q0 + 2 * D_MODEL // N_GROUPS]))

        ti1 = 1.0 + jnp.tanh(gates[:, :GROUP] + 0.5 * gxb_ref[:, cs])
        tr1 = 1.0 + jnp.tanh(gates[:, GROUP:] + 0.5 * gab_ref[:, cs])
        nlam = -lam_ref[:, cs]
        softplus = jnp.maximum(nlam, 0.0) + jnp.log1p(jnp.exp(-jnp.abs(nlam)))
        nla = tr1 * ((0.5 * LRU_C) * softplus)
        a = jnp.exp(-nla)
        x4 = jnp.tanh(nla) * (a * (0.25 * a) + 0.25)
        mult = x4 * lax.rsqrt(jnp.maximum(x4, TINY))
        u = mult * (ti1 * xc)

        a3 = a.reshape(n_rg, SUBLANES, GROUP)
        u3 = u.reshape(n_rg, SUBLANES, GROUP)
        for s in (1, 2, 4):
            valid = sub >= s
            u_sh = jnp.where(valid, pltpu.roll(u3, s, 1), 0.0)
            a_sh = jnp.where(valid, pltpu.roll(a3, s, 1), 1.0)
            u3 = u3 + a3 * u_sh
            a3 = a3 * a_sh
        carry = hst_ref[:, cs]
        for j in range(n_rg):
            hj = u3[j] + a3[j] * carry
            lru_ref[j * SUBLANES:(j + 1) * SUBLANES, :] = hj
            carry = jnp.broadcast_to(hj[SUBLANES - 1:SUBLANES, :], (SUBLANES, GROUP))
        hst_ref[:, cs] = carry
        ya_ref[:, cs] = (ga * (1.0 + jnp.tanh(ga)) * lru_ref[...]).astype(bf16)

        silu_gb = gb * (1.0 + jnp.tanh(gb))
        gain = gn_ref[g:g + 1, :]
        r_state = r_ref[g]
        for c in range(N_CHUNKS):
            rows = slice(c * CHUNK, (c + 1) * CHUNK)
            cross = _dot(qr[rows], r_state.astype(bf16)) * both_halves(cdec_ref[g])
            r_state = chdec_ref[:, cs] * r_state + kv[c]
            inner = _dot((scores[c] * idec_ref[g]).astype(bf16), v_bf[rows])
            y = inner + cross
            mu = jnp.mean(y, axis=-1, keepdims=True)
            yc = y - mu
            var = jnp.mean(yc * yc, axis=-1, keepdims=True)
            gn = yc * lax.rsqrt(var + EPS) * gain
            yb_ref[rows, cs] = (silu_gb[rows] * gn).astype(bf16)
        r_ref[g] = r_state

    oa = oa_head + _dot(ya_ref[:, done:], wpa_s[done:, :])
    ob = ob_head + _dot(yb_ref[:, done:], wpb_s[done:, :])
    mg = jnp.concatenate(mg_parts, axis=1)
    merged = ((1.0 + jnp.tanh(mg[:, :D_MODEL])) * oa
              + (1.0 + jnp.tanh(mg[:, D_MODEL:])) * ob).astype(bf16)
    out_ref[0, pl.ds(row0, T), :] = _dot(merged, wout_s[...])
    g_fin = nfin_ref[...]
    for r0 in range(0, T, ROW_CHUNK):
        rows = pl.ds(row0 + r0, ROW_CHUNK)
        xo = x_ref[0, rows, :] + out_ref[0, rows, :]
        if final_norm:
            ms = jnp.mean(xo * xo, axis=-1, keepdims=True)
            xo = xo * lax.rsqrt(ms + EPS) * g_fin
        out_ref[0, rows, :] = xo


def _block_diag_groups(w):
    per = GROUP // LRU_BW
    tiled = jnp.tile(w.reshape(N_GROUPS, GROUP, LRU_BW), (1, 1, per))
    row_blk = lax.broadcasted_iota(jnp.int32, (GROUP, GROUP), 0) // LRU_BW
    col_blk = lax.broadcasted_iota(jnp.int32, (GROUP, GROUP), 1) // LRU_BW
    return jnp.where(row_blk == col_blk, tiled, 0.0)


def _retention_tables(seq):
    half = RET_DK // 2
    freqs = ROPE_THETA ** (-np.arange(half, dtype=np.float64) / half)
    ang = np.arange(seq, dtype=np.float64)[:, None] * freqs[None, :]
    log_g = np.log1p(-(2.0 ** (-5.0 - np.arange(RET_HEADS, dtype=np.float64))))
    idx = np.arange(CHUNK, dtype=np.float64)
    diff = idx[:, None] - idx[None, :]
    inner = np.where(diff >= 0, np.exp(np.maximum(diff, 0.0)[None] * log_g[:, None, None]), 0.0)
    cross = np.exp((idx[:, None] + 1.0) * log_g[None, :])
    state = np.exp((CHUNK - 1.0 - idx[:, None]) * log_g[None, :])
    chunk = np.exp(CHUNK * log_g)[None, :]
    per_head = lambda m: np.broadcast_to(m.T[:, :, None], (RET_HEADS, CHUNK, LANES))
    k_scale = RET_DK ** -0.5
    assert k_scale == 2.0 ** -4
    tables = (np.cos(ang), np.sin(ang), inner * k_scale, per_head(cross),
              per_head(state * k_scale), np.repeat(chunk, RET_DV, axis=1))
    return tuple(jnp.asarray(np.ascontiguousarray(t), dtype=jnp.float32) for t in tables)


def _const_spec(shape):
    zeros = (0,) * len(shape)
    return pl.BlockSpec(shape, lambda b, t: zeros, pipeline_mode=pl.Buffered(1))


def _layer(x, nin, win, cw, cb, gxw, gxb, gaw, gab, lam, gn, wpa, wpb, wout, nfin, tables,
           final_norm):
    B, S, D = x.shape
    assert D == D_MODEL and S % STEP_T == 0
    assert win.shape == (D, N_KINDS * D) and wpa.shape == wpb.shape == wout.shape == (D, D)
    bf16 = jnp.bfloat16
    row = lambda v: v.reshape(1, -1)
    gw = (0.5 * jnp.concatenate([_block_diag_groups(gxw), _block_diag_groups(gaw)], axis=-1)
          ).astype(bf16)
    cos, sin, idec, cdec, sdec, chdec = tables
    operands = (
        x, row(nin), win, cw, row(cb), gw, row(gxb), row(gab), row(lam), gn,
        wpa, wpb, wout, row(nfin), cos, sin, idec, cdec, sdec, chdec)
    hbm_operands = (2, 10, 11, 12)
    tile_spec = pl.BlockSpec((1, STEP_T, D), lambda b, t: (b, t, 0))
    in_specs = []
    for i, o in enumerate(operands):
        if i == 0:
            in_specs.append(tile_spec)
        elif i in hbm_operands:
            in_specs.append(pl.BlockSpec(memory_space=pl.ANY))
        else:
            in_specs.append(_const_spec(o.shape))
    scratch = [
        pltpu.VMEM((D, N_KINDS * D), bf16),
        pltpu.VMEM((D, D), bf16),
        pltpu.VMEM((D, D), bf16),
        pltpu.VMEM((D, D), bf16),
        pltpu.VMEM((N_GROUPS, GROUP, 2 * GROUP), bf16),
        pltpu.VMEM((STAGE_SLOTS, D, STAGE_COLS), jnp.float32),
        pltpu.SemaphoreType.DMA((STAGE_SLOTS,)),
        pltpu.VMEM((TILE_T, D), bf16),
        pltpu.VMEM((D // LANES, TILE_T + HALO, LANES), jnp.float32),
        pltpu.VMEM((TILE_T, GROUP), jnp.float32),
        pltpu.VMEM((TILE_T, D), bf16),
        pltpu.VMEM((TILE_T, D), bf16),
        pltpu.VMEM((RET_HEADS, RET_DK, RET_DV), jnp.float32),
        pltpu.VMEM((SUBLANES, D), jnp.float32),
    ]
    return pl.pallas_call(
        functools.partial(_block_kernel, final_norm=final_norm),
        grid=(B, S // STEP_T),
        in_specs=in_specs,
        out_specs=tile_spec,
        out_shape=jax.ShapeDtypeStruct(x.shape, x.dtype),
        scratch_shapes=scratch,
        compiler_params=pltpu.CompilerParams(
            dimension_semantics=("arbitrary", "arbitrary"),
            vmem_limit_bytes=VMEM_LIMIT_BYTES),
        name="hybrid_block",
    )(*operands)


def kernel(x, norm_in, w_in, conv_w, conv_b, gate_x_w, gate_x_b, gate_a_w, gate_a_b, lru_lambda, gn_gain, w_proj_a, w_proj_b, w_out, norm_final):
    depth = w_in.shape[0]
    tables = _retention_tables(x.shape[1])
    for l in range(depth):
        x = _layer(x, norm_in[l], w_in[l], conv_w[l], conv_b[l], gate_x_w[l], gate_x_b[l],
                   gate_a_w[l], gate_a_b[l], lru_lambda[l], gn_gain[l],
                   w_proj_a[l], w_proj_b[l], w_out[l], norm_final, tables,
                   final_norm=(l == depth - 1))
    return x
```

```python
import functools

import jax
import jax.numpy as jnp
import numpy as np
from jax import lax
from jax.experimental import pallas as pl
from jax.experimental.pallas import tpu as pltpu

D_MODEL = 1024
LRU_BLOCKS = 16
LRU_BW = D_MODEL // LRU_BLOCKS
CONV_WIDTH = 4
LRU_C = 8.0
RET_HEADS = 4
RET_DK = 256
RET_DV = 256
CHUNK = 256
ROPE_THETA = 10000.0
EPS = 1e-6

KIND_XA, KIND_GA, KIND_Q, KIND_K, KIND_V, KIND_GB, KIND_MA, KIND_MB = range(8)
N_KINDS = 8
N_GROUP_KINDS = 6
HALVED_KINDS = (KIND_GA, KIND_GB, KIND_MA, KIND_MB)

LANES = 128
SUBLANES = 8
GROUP = 256
N_GROUPS = D_MODEL // GROUP
GROUP_COLS = N_GROUP_KINDS * GROUP
MERGE_COL0 = N_GROUPS * GROUP_COLS
TILE_T = 512
TILES_PER_STEP = 1
STEP_T = TILE_T * TILES_PER_STEP
N_CHUNKS = TILE_T // CHUNK
ROW_CHUNK = 32
STAGE_COLS = 256
STAGE_SLOTS = 4
PREP_ROWS = 64
HALO = SUBLANES
VMEM_LIMIT_BYTES = 56 * 1024 * 1024
TINY = 1e-37


def _dot(a, b):
    return jnp.dot(a, b, preferred_element_type=jnp.float32)


def _prepare_weights(win_hbm, wpa_hbm, wpb_hbm, wout_hbm, win_s, wpa_s, wpb_s, wout_s,
                     stage, sem):
    slabs = []
    for kind in range(N_KINDS):
        for c0 in range(0, D_MODEL, STAGE_COLS):
            src = win_hbm.at[:, pl.ds(kind * D_MODEL + c0, STAGE_COLS)]
            if kind < N_GROUP_KINDS:
                pieces = [(c - c0, (c // GROUP) * GROUP_COLS + kind * GROUP, GROUP)
                          for c in range(c0, c0 + STAGE_COLS, GROUP)]
            else:
                pieces = [(0, kind * D_MODEL + c0, STAGE_COLS)]
            slabs.append((src, win_s, pieces, kind in HALVED_KINDS))
    for src_hbm, dst, halve in ((wpa_hbm, wpa_s, True), (wpb_hbm, wpb_s, True),
                                (wout_hbm, wout_s, False)):
        for c0 in range(0, D_MODEL, STAGE_COLS):
            slabs.append((src_hbm.at[:, pl.ds(c0, STAGE_COLS)], dst,
                          [(0, c0, STAGE_COLS)], halve))

    def copy(n):
        slot = n % STAGE_SLOTS
        return pltpu.make_async_copy(slabs[n][0], stage.at[slot], sem.at[slot])

    for n in range(STAGE_SLOTS - 1):
        copy(n).start()
    for n, (_, dst, pieces, halve) in enumerate(slabs):
        if n + STAGE_SLOTS - 1 < len(slabs):
            copy(n + STAGE_SLOTS - 1).start()
        copy(n).wait()

        def convert(i, carry, slot=n % STAGE_SLOTS, dst=dst, pieces=pieces, halve=halve):
            rows = pl.ds(pl.multiple_of(i * PREP_ROWS, PREP_ROWS), PREP_ROWS)
            w = stage[slot, rows, :]
            if halve:
                w = 0.5 * w
            w = w.astype(jnp.bfloat16)
            for s0, d0, width in pieces:
                dst[rows, d0:d0 + width] = w[:, s0:s0 + width]
            return carry

        lax.fori_loop(0, D_MODEL // PREP_ROWS, convert, 0)


def _block_kernel(x_ref, nin_ref, win_hbm, cw_ref, cb_ref, gw_ref, gxb_ref, gab_ref, lam_ref,
                  gn_ref, wpa_hbm, wpb_hbm, wout_hbm, nfin_ref, cos_ref, sin_ref, idec_ref,
                  cdec_ref, sdec_ref, chdec_ref, out_ref,
                  win_s, wpa_s, wpb_s, wout_s, gw_s, stage, sem,
                  h_ref, xa_ref, lru_ref, ya_ref, yb_ref, r_ref, hst_ref,
                  *, final_norm):
    @pl.when((pl.program_id(0) == 0) & (pl.program_id(1) == 0))
    def _():
        _prepare_weights(win_hbm, wpa_hbm, wpb_hbm, wout_hbm, win_s, wpa_s, wpb_s, wout_s,
                         stage, sem)
        for g in range(N_GROUPS):
            gw_s[g] = gw_ref[g]

    def tile(i, row0):
        _tile_body(pl.program_id(1) * TILES_PER_STEP + i, row0,
                   x_ref, nin_ref, cw_ref, cb_ref, gw_s, gxb_ref, gab_ref, lam_ref, gn_ref,
                   nfin_ref, cos_ref, sin_ref, idec_ref, cdec_ref, sdec_ref, chdec_ref, out_ref,
                   win_s, wpa_s, wpb_s, wout_s, h_ref, xa_ref, lru_ref, ya_ref, yb_ref, r_ref,
                   hst_ref, final_norm)

    for i in range(TILES_PER_STEP):
        tile(i, i * TILE_T)


def _tile_body(t, row0, x_ref, nin_ref, cw_ref, cb_ref, gw_ref, gxb_ref, gab_ref, lam_ref, gn_ref,
               nfin_ref, cos_ref, sin_ref, idec_ref, cdec_ref, sdec_ref, chdec_ref, out_ref,
               win_s, wpa_s, wpb_s, wout_s, h_ref, xa_ref, lru_ref, ya_ref, yb_ref, r_ref,
               hst_ref, final_norm):
    T = TILE_T
    bf16 = jnp.bfloat16

    @pl.when(t == 0)
    def _():
        r_ref[...] = jnp.zeros_like(r_ref)
        hst_ref[...] = jnp.zeros_like(hst_ref)
        xa_ref[:, 0:HALO, :] = jnp.zeros((D_MODEL // LANES, HALO, LANES), jnp.float32)

    g_in = nin_ref[...]
    for r0 in range(0, T, ROW_CHUNK):
        xr = x_ref[0, pl.ds(row0 + r0, ROW_CHUNK), :]
        ms = jnp.mean(xr * xr, axis=-1, keepdims=True)
        h_ref[r0:r0 + ROW_CHUNK, :] = (xr * lax.rsqrt(ms + EPS) * g_in).astype(bf16)

    n_rg = T // SUBLANES
    sub = lax.broadcasted_iota(jnp.int32, (n_rg, SUBLANES, GROUP), 1)
    half = RET_DK // 2
    both_halves = lambda m: jnp.concatenate([m, m], axis=1)

    def project(g):
        return _dot(h_ref[...], win_s[:, g * GROUP_COLS:(g + 1) * GROUP_COLS])

    proj = project(0)
    for g in range(N_GROUPS):
        cs = slice(g * GROUP, (g + 1) * GROUP)
        kind = lambda i, p=proj: p[:, i * GROUP:(i + 1) * GROUP]
        if g + 1 == N_GROUPS:
            done = g * GROUP
            oa_head = _dot(ya_ref[:, :done], wpa_s[:done, :])
            ob_head = _dot(yb_ref[:, :done], wpb_s[:done, :])

        xa = kind(KIND_XA)
        xc_blocks = []
        for b in range(GROUP // LANES):
            blk = g * (GROUP // LANES) + b
            lc = slice(blk * LANES, (blk + 1) * LANES)
            xa_b = xa[:, b * LANES:(b + 1) * LANES]
            xa_ref[blk, HALO:HALO + T, :] = xa_b
            xc_b = cb_ref[:, lc] + cw_ref[CONV_WIDTH - 1:CONV_WIDTH, lc] * xa_b
            for k in range(CONV_WIDTH - 1):
                lo = HALO - (CONV_WIDTH - 1) + k
                xc_b = xc_b + cw_ref[k:k + 1, lc] * xa_ref[blk, lo:lo + T, :]
            xa_ref[blk, 0:HALO, :] = xa_ref[blk, T:T + HALO, :]
            xc_blocks.append(xc_b)
        xc = jnp.concatenate(xc_blocks, axis=1)
        gates = _dot(xc.astype(bf16), gw_ref[g])

        q, k, v, gb = kind(KIND_Q), kind(KIND_K), kind(KIND_V), kind(KIND_GB)
        ga = kind(KIND_GA)
        pos = pl.ds(pl.multiple_of(t * T, T), T)
        cos = cos_ref[pos, :]
        sin = sin_ref[pos, :]
        q1, q2 = q[:, :half], q[:, half:]
        k1, k2 = k[:, :half], k[:, half:]
        qr = jnp.concatenate([q1 * cos - q2 * sin, q1 * sin + q2 * cos], axis=1).astype(bf16)
        kr = jnp.concatenate([k1 * cos - k2 * sin, k1 * sin + k2 * cos], axis=1).astype(bf16)
        v_bf = v.astype(bf16)
        scores, kv = [], []
        for c in range(N_CHUNKS):
            rows = slice(c * CHUNK, (c + 1) * CHUNK)
            scores.append(lax.dot_general(qr[rows], kr[rows], (((1,), (1,)), ((), ())),
                                          preferred_element_type=jnp.float32))
            kv.append(lax.dot_general(kr[rows], (v[rows] * both_halves(sdec_ref[g])).astype(bf16),
                                      (((0,), (0,)), ((), ())),
                                      preferred_element_type=jnp.float32))

        if g + 1 < N_GROUPS:
            proj = project(g + 1)
        else:
            mg = _dot(h_ref[...], win_s[:, MERGE_COL0:MERGE_COL0 + 2 * D_MODEL])

        ti1 = 1.0 + jnp.tanh(gates[:, :GROUP] + 0.5 * gxb_ref[:, cs])
        tr1 = 1.0 + jnp.tanh(gates[:, GROUP:] + 0.5 * gab_ref[:, cs])
        nlam = -lam_ref[:, cs]
        softplus = jnp.maximum(nlam, 0.0) + jnp.log1p(jnp.exp(-jnp.abs(nlam)))
        nla = tr1 * ((0.5 * LRU_C) * softplus)
        a = jnp.exp(-nla)
        x4 = jnp.tanh(nla) * (a * (0.25 * a) + 0.25)
        mult = x4 * lax.rsqrt(jnp.maximum(x4, TINY))
        u = mult * (ti1 * xc)

        a3 = a.reshape(n_rg, SUBLANES, GROUP)
        u3 = u.reshape(n_rg, SUBLANES, GROUP)
        for s in (1, 2, 4):
            valid = sub >= s
            u_sh = jnp.where(valid, pltpu.roll(u3, s, 1), 0.0)
            a_sh = jnp.where(valid, pltpu.roll(a3, s, 1), 1.0)
            u3 = u3 + a3 * u_sh
            a3 = a3 * a_sh
        carry = hst_ref[:, cs]
        for j in range(n_rg):
            hj = u3[j] + a3[j] * carry
            lru_ref[j * SUBLANES:(j + 1) * SUBLANES, :] = hj
            carry = jnp.broadcast_to(hj[SUBLANES - 1:SUBLANES, :], (SUBLANES, GROUP))
        hst_ref[:, cs] = carry
        ya_ref[:, cs] = (ga * (1.0 + jnp.tanh(ga)) * lru_ref[...]).astype(bf16)

        silu_gb = gb * (1.0 + jnp.tanh(gb))
        gain = gn_ref[g:g + 1, :]
        r_state = r_ref[g]
        for c in range(N_CHUNKS):
            rows = slice(c * CHUNK, (c + 1) * CHUNK)
            cross = _dot(qr[rows], r_state.astype(bf16)) * both_halves(cdec_ref[g])
            r_state = chdec_ref[:, cs] * r_state + kv[c]
            inner = _dot((scores[c] * idec_ref[g]).astype(bf16), v_bf[rows])
            y = inner + cross
            mu = jnp.mean(y, axis=-1, keepdims=True)
            yc = y - mu
            var = jnp.mean(yc * yc, axis=-1, keepdims=True)
            gn = yc * lax.rsqrt(var + EPS) * gain
            yb_ref[rows, cs] = (silu_gb[rows] * gn).astype(bf16)
        r_ref[g] = r_state

    oa = oa_head + _dot(ya_ref[:, done:], wpa_s[done:, :])
    ob = ob_head + _dot(yb_ref[:, done:], wpb_s[done:, :])
    merged = ((1.0 + jnp.tanh(mg[:, :D_MODEL])) * oa
              + (1.0 + jnp.tanh(mg[:, D_MODEL:])) * ob).astype(bf16)
    out_ref[0, pl.ds(row0, T), :] = _dot(merged, wout_s[...])
    g_fin = nfin_ref[...]
    for r0 in range(0, T, ROW_CHUNK):
        rows = pl.ds(row0 + r0, ROW_CHUNK)
        xo = x_ref[0, rows, :] + out_ref[0, rows, :]
        if final_norm:
            ms = jnp.mean(xo * xo, axis=-1, keepdims=True)
            xo = xo * lax.rsqrt(ms + EPS) * g_fin
        out_ref[0, rows, :] = xo


def _block_diag_groups(w):
    per = GROUP // LRU_BW
    tiled = jnp.tile(w, (1, 1, per)).reshape(N_GROUPS, GROUP, GROUP)
    row_blk = lax.broadcasted_iota(jnp.int32, (GROUP, GROUP), 0) // LRU_BW
    col_blk = lax.broadcasted_iota(jnp.int32, (GROUP, GROUP), 1) // LRU_BW
    return jnp.where(row_blk == col_blk, tiled, 0.0)


def _retention_tables(seq):
    half = RET_DK // 2
    freqs = ROPE_THETA ** (-np.arange(half, dtype=np.float64) / half)
    ang = np.arange(seq, dtype=np.float64)[:, None] * freqs[None, :]
    log_g = np.log1p(-(2.0 ** (-5.0 - np.arange(RET_HEADS, dtype=np.float64))))
    idx = np.arange(CHUNK, dtype=np.float64)
    diff = idx[:, None] - idx[None, :]
    inner = np.where(diff >= 0, np.exp(np.maximum(diff, 0.0)[None] * log_g[:, None, None]), 0.0)
    cross = np.exp((idx[:, None] + 1.0) * log_g[None, :])
    state = np.exp((CHUNK - 1.0 - idx[:, None]) * log_g[None, :])
    chunk = np.exp(CHUNK * log_g)[None, :]
    per_head = lambda m: np.broadcast_to(m.T[:, :, None], (RET_HEADS, CHUNK, LANES))
    k_scale = RET_DK ** -0.5
    assert k_scale == 2.0 ** -4
    tables = (np.cos(ang), np.sin(ang), inner * k_scale, per_head(cross),
              per_head(state * k_scale), np.repeat(chunk, RET_DV, axis=1))
    return tuple(jnp.asarray(np.ascontiguousarray(t), dtype=jnp.float32) for t in tables)


def _const_spec(shape):
    zeros = (0,) * len(shape)
    return pl.BlockSpec(shape, lambda b, t: zeros, pipeline_mode=pl.Buffered(1))


def _layer(x, nin, win, cw, cb, gxw, gxb, gaw, gab, lam, gn, wpa, wpb, wout, nfin, tables,
           final_norm):
    B, S, D = x.shape
    assert D == D_MODEL and S % STEP_T == 0
    assert win.shape == (D, N_KINDS * D) and wpa.shape == wpb.shape == wout.shape == (D, D)
    bf16 = jnp.bfloat16
    row = lambda v: v.reshape(1, -1)
    gw = (0.5 * jnp.concatenate([_block_diag_groups(gxw), _block_diag_groups(gaw)], axis=-1)
          ).astype(bf16)
    cos, sin, idec, cdec, sdec, chdec = tables
    operands = (
        x, row(nin), win, cw, row(cb), gw, row(gxb), row(gab), row(lam), gn,
        wpa, wpb, wout, row(nfin), cos, sin, idec, cdec, sdec, chdec)
    hbm_operands = (2, 10, 11, 12)
    tile_spec = pl.BlockSpec((1, STEP_T, D), lambda b, t: (b, t, 0))
    in_specs = []
    for i, o in enumerate(operands):
        if i == 0:
            in_specs.append(tile_spec)
        elif i in hbm_operands:
            in_specs.append(pl.BlockSpec(memory_space=pl.ANY))
        else:
            in_specs.append(_const_spec(o.shape))
    scratch = [
        pltpu.VMEM((D, N_KINDS * D), bf16),
        pltpu.VMEM((D, D), bf16),
        pltpu.VMEM((D, D), bf16),
        pltpu.VMEM((D, D), bf16),
        pltpu.VMEM((N_GROUPS, GROUP, 2 * GROUP), bf16),
        pltpu.VMEM((STAGE_SLOTS, D, STAGE_COLS), jnp.float32),
        pltpu.SemaphoreType.DMA((STAGE_SLOTS,)),
        pltpu.VMEM((TILE_T, D), bf16),
        pltpu.VMEM((D // LANES, TILE_T + HALO, LANES), jnp.float32),
        pltpu.VMEM((TILE_T, GROUP), jnp.float32),
        pltpu.VMEM((TILE_T, D), bf16),
        pltpu.VMEM((TILE_T, D), bf16),
        pltpu.VMEM((RET_HEADS, RET_DK, RET_DV), jnp.float32),
        pltpu.VMEM((SUBLANES, D), jnp.float32),
    ]
    return pl.pallas_call(
        functools.partial(_block_kernel, final_norm=final_norm),
        grid=(B, S // STEP_T),
        in_specs=in_specs,
        out_specs=tile_spec,
        out_shape=jax.ShapeDtypeStruct(x.shape, x.dtype),
        scratch_shapes=scratch,
        compiler_params=pltpu.CompilerParams(
            dimension_semantics=("arbitrary", "arbitrary"),
            vmem_limit_bytes=VMEM_LIMIT_BYTES),
        name="hybrid_block",
    )(*operands)


def kernel(x, norm_in, w_in, conv_w, conv_b, gate_x_w, gate_x_b, gate_a_w, gate_a_b, lru_lambda, gn_gain, w_proj_a, w_proj_b, w_out, norm_final):
    depth = w_in.shape[0]
    tables = _retention_tables(x.shape[1])
    for l in range(depth):
        x = _layer(x, norm_in[l], w_in[l], conv_w[l], conv_b[l], gate_x_w[l], gate_x_b[l],
                   gate_a_w[l], gate_a_b[l], lru_lambda[l], gn_gain[l],
                   w_proj_a[l], w_proj_b[l], w_out[l], norm_final, tables,
                   final_norm=(l == depth - 1))
    return x
```

```python
import functools

import jax
import jax.numpy as jnp
import numpy as np
from jax import lax
from jax.experimental import pallas as pl
from jax.experimental.pallas import tpu as pltpu

D_MODEL = 1024
LRU_BLOCKS = 16
LRU_BW = D_MODEL // LRU_BLOCKS
CONV_WIDTH = 4
LRU_C = 8.0
RET_HEADS = 4
RET_DK = 256
RET_DV = 256
CHUNK = 256
ROPE_THETA = 10000.0
EPS = 1e-6

KIND_XA, KIND_GA, KIND_Q, KIND_K, KIND_V, KIND_GB, KIND_MA, KIND_MB = range(8)
N_KINDS = 8
N_GROUP_KINDS = 6
HALVED_KINDS = (KIND_GA, KIND_GB, KIND_MA, KIND_MB)

LANES = 128
SUBLANES = 8
GROUP = 256
N_GROUPS = D_MODEL // GROUP
GROUP_COLS = N_GROUP_KINDS * GROUP
MERGE_COL0 = N_GROUPS * GROUP_COLS
TILE_T = 512
TILES_PER_STEP = 1
STEP_T = TILE_T * TILES_PER_STEP
N_CHUNKS = TILE_T // CHUNK
ROW_CHUNK = 32
STAGE_COLS = 256
STAGE_SLOTS = 4
PREP_ROWS = 64
HALO = SUBLANES
VMEM_LIMIT_BYTES = 56 * 1024 * 1024
TINY = 1e-37


def _dot(a, b):
    return jnp.dot(a, b, preferred_element_type=jnp.float32)


def _prepare_weights(win_hbm, wpa_hbm, wpb_hbm, wout_hbm, win_s, wpa_s, wpb_s, wout_s,
                     stage, sem):
    slabs = []
    for kind in range(N_KINDS):
        for c0 in range(0, D_MODEL, STAGE_COLS):
            src = win_hbm.at[:, pl.ds(kind * D_MODEL + c0, STAGE_COLS)]
            if kind < N_GROUP_KINDS:
                pieces = [(c - c0, (c // GROUP) * GROUP_COLS + kind * GROUP, GROUP)
                          for c in range(c0, c0 + STAGE_COLS, GROUP)]
            else:
                pieces = [(0, kind * D_MODEL + c0, STAGE_COLS)]
            slabs.append((src, win_s, pieces, kind in HALVED_KINDS))
    for src_hbm, dst, halve in ((wpa_hbm, wpa_s, True), (wpb_hbm, wpb_s, True),
                                (wout_hbm, wout_s, False)):
        for c0 in range(0, D_MODEL, STAGE_COLS):
            slabs.append((src_hbm.at[:, pl.ds(c0, STAGE_COLS)], dst,
                          [(0, c0, STAGE_COLS)], halve))

    def copy(n):
        slot = n % STAGE_SLOTS
        return pltpu.make_async_copy(slabs[n][0], stage.at[slot], sem.at[slot])

    for n in range(STAGE_SLOTS - 1):
        copy(n).start()
    for n, (_, dst, pieces, halve) in enumerate(slabs):
        if n + STAGE_SLOTS - 1 < len(slabs):
            copy(n + STAGE_SLOTS - 1).start()
        copy(n).wait()

        def convert(i, carry, slot=n % STAGE_SLOTS, dst=dst, pieces=pieces, halve=halve):
            rows = pl.ds(pl.multiple_of(i * PREP_ROWS, PREP_ROWS), PREP_ROWS)
            w = stage[slot, rows, :]
            if halve:
                w = 0.5 * w
            w = w.astype(jnp.bfloat16)
            for s0, d0, width in pieces:
                dst[rows, d0:d0 + width] = w[:, s0:s0 + width]
            return carry

        lax.fori_loop(0, D_MODEL // PREP_ROWS, convert, 0)


def _block_kernel(x_ref, nin_ref, win_hbm, cw_ref, cb_ref, gw_ref, gxb_ref, gab_ref, lam_ref,
                  gn_ref, wpa_hbm, wpb_hbm, wout_hbm, nfin_ref, cos_ref, sin_ref, idec_ref,
                  cdec_ref, sdec_ref, chdec_ref, out_ref,
                  win_s, wpa_s, wpb_s, wout_s, gw_s, stage, sem,
                  h_ref, xa_ref, lru_ref, ya_ref, yb_ref, r_ref, hst_ref,
                  *, final_norm):
    @pl.when((pl.program_id(0) == 0) & (pl.program_id(1) == 0))
    def _():
        _prepare_weights(win_hbm, wpa_hbm, wpb_hbm, wout_hbm, win_s, wpa_s, wpb_s, wout_s,
                         stage, sem)
        for g in range(N_GROUPS):
            gw_s[g] = gw_ref[g]

    def tile(i, row0):
        _tile_body(pl.program_id(1) * TILES_PER_STEP + i, row0,
                   x_ref, nin_ref, cw_ref, cb_ref, gw_s, gxb_ref, gab_ref, lam_ref, gn_ref,
                   nfin_ref, cos_ref, sin_ref, idec_ref, cdec_ref, sdec_ref, chdec_ref, out_ref,
                   win_s, wpa_s, wpb_s, wout_s, h_ref, xa_ref, lru_ref, ya_ref, yb_ref, r_ref,
                   hst_ref, final_norm)

    for i in range(TILES_PER_STEP):
        tile(i, i * TILE_T)


def _tile_body(t, row0, x_ref, nin_ref, cw_ref, cb_ref, gw_ref, gxb_ref, gab_ref, lam_ref, gn_ref,
               nfin_ref, cos_ref, sin_ref, idec_ref, cdec_ref, sdec_ref, chdec_ref, out_ref,
               win_s, wpa_s, wpb_s, wout_s, h_ref, xa_ref, lru_ref, ya_ref, yb_ref, r_ref,
               hst_ref, final_norm):
    T = TILE_T
    bf16 = jnp.bfloat16

    @pl.when(t == 0)
    def _():
        r_ref[...] = jnp.zeros_like(r_ref)
        hst_ref[...] = jnp.zeros_like(hst_ref)
        xa_ref[:, 0:HALO, :] = jnp.zeros((D_MODEL // LANES, HALO, LANES), jnp.float32)

    g_in = nin_ref[...]
    for r0 in range(0, T, ROW_CHUNK):
        xr = x_ref[0, pl.ds(row0 + r0, ROW_CHUNK), :]
        ms = jnp.mean(xr * xr, axis=-1, keepdims=True)
        h_ref[r0:r0 + ROW_CHUNK, :] = (xr * lax.rsqrt(ms + EPS) * g_in).astype(bf16)

    n_rg = T // SUBLANES
    sub = lax.broadcasted_iota(jnp.int32, (n_rg, SUBLANES, LANES), 1)
    half = RET_DK // 2
    both_halves = lambda m: jnp.concatenate([m, m], axis=1)

    def project(g):
        return _dot(h_ref[...], win_s[:, g * GROUP_COLS:(g + 1) * GROUP_COLS])

    proj = project(0)
    for g in range(N_GROUPS):
        cs = slice(g * GROUP, (g + 1) * GROUP)
        kind = lambda i, p=proj: p[:, i * GROUP:(i + 1) * GROUP]
        if g + 1 == N_GROUPS:
            done = g * GROUP
            oa_head = _dot(ya_ref[:, :done], wpa_s[:done, :])
            ob_head = _dot(yb_ref[:, :done], wpb_s[:done, :])

        xa = kind(KIND_XA)
        xc_blocks = []
        for b in range(GROUP // LANES):
            blk = g * (GROUP // LANES) + b
            lc = slice(blk * LANES, (blk + 1) * LANES)
            xa_b = xa[:, b * LANES:(b + 1) * LANES]
            xa_ref[blk, HALO:HALO + T, :] = xa_b
            xc_b = cb_ref[:, lc] + cw_ref[CONV_WIDTH - 1:CONV_WIDTH, lc] * xa_b
            for k in range(CONV_WIDTH - 1):
                lo = HALO - (CONV_WIDTH - 1) + k
                xc_b = xc_b + cw_ref[k:k + 1, lc] * xa_ref[blk, lo:lo + T, :]
            xa_ref[blk, 0:HALO, :] = xa_ref[blk, T:T + HALO, :]
            xc_blocks.append(xc_b)
        xc = jnp.concatenate(xc_blocks, axis=1)
        gates = _dot(xc.astype(bf16), gw_ref[g])

        q, k, v, gb = kind(KIND_Q), kind(KIND_K), kind(KIND_V), kind(KIND_GB)
        ga = kind(KIND_GA)
        pos = pl.ds(pl.multiple_of(t * T, T), T)
        cos = cos_ref[pos, :]
        sin = sin_ref[pos, :]
        q1, q2 = q[:, :half], q[:, half:]
        k1, k2 = k[:, :half], k[:, half:]
        qr = jnp.concatenate([q1 * cos - q2 * sin, q1 * sin + q2 * cos], axis=1).astype(bf16)
        kr = jnp.concatenate([k1 * cos - k2 * sin, k1 * sin + k2 * cos], axis=1).astype(bf16)
        v_bf = v.astype(bf16)
        scores, kv = [], []
        for c in range(N_CHUNKS):
            rows = slice(c * CHUNK, (c + 1) * CHUNK)
            scores.append(lax.dot_general(qr[rows], kr[rows], (((1,), (1,)), ((), ())),
                                          preferred_element_type=jnp.float32))
            kv.append(lax.dot_general(kr[rows], (v[rows] * both_halves(sdec_ref[g])).astype(bf16),
                                      (((0,), (0,)), ((), ())),
                                      preferred_element_type=jnp.float32))

        if g + 1 < N_GROUPS:
            proj = project(g + 1)
        else:
            mg = _dot(h_ref[...], win_s[:, MERGE_COL0:MERGE_COL0 + 2 * D_MODEL])

        for b in range(GROUP // LANES):
            hl = slice(b * LANES, (b + 1) * LANES)
            gl = slice(g * GROUP + b * LANES, g * GROUP + (b + 1) * LANES)
            ti1 = 1.0 + jnp.tanh(gates[:, hl] + 0.5 * gxb_ref[:, gl])
            tr1 = 1.0 + jnp.tanh(gates[:, GROUP + b * LANES:GROUP + (b + 1) * LANES]
                                 + 0.5 * gab_ref[:, gl])
            nlam = -lam_ref[:, gl]
            softplus = jnp.maximum(nlam, 0.0) + jnp.log1p(jnp.exp(-jnp.abs(nlam)))
            nla = tr1 * ((0.5 * LRU_C) * softplus)
            a = jnp.exp(-nla)
            x4 = jnp.tanh(nla) * (a * (0.25 * a) + 0.25)
            mult = x4 * lax.rsqrt(jnp.maximum(x4, TINY))
            u = mult * (ti1 * xc[:, hl])

            a3 = a.reshape(n_rg, SUBLANES, LANES)
            u3 = u.reshape(n_rg, SUBLANES, LANES)
            for s in (1, 2, 4):
                valid = sub >= s
                u_sh = jnp.where(valid, pltpu.roll(u3, s, 1), 0.0)
                a_sh = jnp.where(valid, pltpu.roll(a3, s, 1), 1.0)
                u3 = u3 + a3 * u_sh
                a3 = a3 * a_sh
            carry = hst_ref[:, gl]
            for j in range(n_rg):
                hj = u3[j] + a3[j] * carry
                lru_ref[j * SUBLANES:(j + 1) * SUBLANES, hl] = hj
                carry = jnp.broadcast_to(hj[SUBLANES - 1:SUBLANES, :], (SUBLANES, LANES))
            hst_ref[:, gl] = carry
        ya_ref[:, cs] = (ga * (1.0 + jnp.tanh(ga)) * lru_ref[...]).astype(bf16)

        silu_gb = gb * (1.0 + jnp.tanh(gb))
        gain = gn_ref[g:g + 1, :]
        r_state = r_ref[g]
        for c in range(N_CHUNKS):
            rows = slice(c * CHUNK, (c + 1) * CHUNK)
            cross = _dot(qr[rows], r_state.astype(bf16)) * both_halves(cdec_ref[g])
            r_state = chdec_ref[:, cs] * r_state + kv[c]
            inner = _dot((scores[c] * idec_ref[g]).astype(bf16), v_bf[rows])
            y = inner + cross
            mu = jnp.mean(y, axis=-1, keepdims=True)
            yc = y - mu
            var = jnp.mean(yc * yc, axis=-1, keepdims=True)
            gn = yc * lax.rsqrt(var + EPS) * gain
            yb_ref[rows, cs] = (silu_gb[rows] * gn).astype(bf16)
        r_ref[g] = r_state

    oa = oa_head + _dot(ya_ref[:, done:], wpa_s[done:, :])
    ob = ob_head + _dot(yb_ref[:, done:], wpb_s[done:, :])
    merged = ((1.0 + jnp.tanh(mg[:, :D_MODEL])) * oa
              + (1.0 + jnp.tanh(mg[:, D_MODEL:])) * ob).astype(bf16)
    out_ref[0, pl.ds(row0, T), :] = _dot(merged, wout_s[...])
    g_fin = nfin_ref[...]
    for r0 in range(0, T, ROW_CHUNK):
        rows = pl.ds(row0 + r0, ROW_CHUNK)
        xo = x_ref[0, rows, :] + out_ref[0, rows, :]
        if final_norm:
            ms = jnp.mean(xo * xo, axis=-1, keepdims=True)
            xo = xo * lax.rsqrt(ms + EPS) * g_fin
        out_ref[0, rows, :] = xo


def _block_diag_groups(w):
    per = GROUP // LRU_BW
    tiled = jnp.tile(w, (1, 1, per)).reshape(N_GROUPS, GROUP, GROUP)
    row_blk = lax.broadcasted_iota(jnp.int32, (GROUP, GROUP), 0) // LRU_BW
    col_blk = lax.broadcasted_iota(jnp.int32, (GROUP, GROUP), 1) // LRU_BW
    return jnp.where(row_blk == col_blk, tiled, 0.0)


def _retention_tables(seq):
    half = RET_DK // 2
    freqs = ROPE_THETA ** (-np.arange(half, dtype=np.float64) / half)
    ang = np.arange(seq, dtype=np.float64)[:, None] * freqs[None, :]
    log_g = np.log1p(-(2.0 ** (-5.0 - np.arange(RET_HEADS, dtype=np.float64))))
    idx = np.arange(CHUNK, dtype=np.float64)
    diff = idx[:, None] - idx[None, :]
    inner = np.where(diff >= 0, np.exp(np.maximum(diff, 0.0)[None] * log_g[:, None, None]), 0.0)
    cross = np.exp((idx[:, None] + 1.0) * log_g[None, :])
    state = np.exp((CHUNK - 1.0 - idx[:, None]) * log_g[None, :])
    chunk = np.exp(CHUNK * log_g)[None, :]
    per_head = lambda m: np.broadcast_to(m.T[:, :, None], (RET_HEADS, CHUNK, LANES))
    k_scale = RET_DK ** -0.5
    assert k_scale == 2.0 ** -4
    tables = (np.cos(ang), np.sin(ang), inner * k_scale, per_head(cross),
              per_head(state * k_scale), np.repeat(chunk, RET_DV, axis=1))
    return tuple(jnp.asarray(np.ascontiguousarray(t), dtype=jnp.float32) for t in tables)


def _const_spec(shape):
    zeros = (0,) * len(shape)
    return pl.BlockSpec(shape, lambda b, t: zeros, pipeline_mode=pl.Buffered(1))


def _layer(x, nin, win, cw, cb, gxw, gxb, gaw, gab, lam, gn, wpa, wpb, wout, nfin, tables,
           final_norm):
    B, S, D = x.shape
    assert D == D_MODEL and S % STEP_T == 0
    assert win.shape == (D, N_KINDS * D) and wpa.shape == wpb.shape == wout.shape == (D, D)
    bf16 = jnp.bfloat16
    row = lambda v: v.reshape(1, -1)
    gw = (0.5 * jnp.concatenate([_block_diag_groups(gxw), _block_diag_groups(gaw)], axis=-1)
          ).astype(bf16)
    cos, sin, idec, cdec, sdec, chdec = tables
    operands = (
        x, row(nin), win, cw, row(cb), gw, row(gxb), row(gab), row(lam), gn,
        wpa, wpb, wout, row(nfin), cos, sin, idec, cdec, sdec, chdec)
    hbm_operands = (2, 10, 11, 12)
    tile_spec = pl.BlockSpec((1, STEP_T, D), lambda b, t: (b, t, 0))
    in_specs = []
    for i, o in enumerate(operands):
        if i == 0:
            in_specs.append(tile_spec)
        elif i in hbm_operands:
            in_specs.append(pl.BlockSpec(memory_space=pl.ANY))
        else:
            in_specs.append(_const_spec(o.shape))
    scratch = [
        pltpu.VMEM((D, N_KINDS * D), bf16),
        pltpu.VMEM((D, D), bf16),
        pltpu.VMEM((D, D), bf16),
        pltpu.VMEM((D, D), bf16),
        pltpu.VMEM((N_GROUPS, GROUP, 2 * GROUP), bf16),
        pltpu.VMEM((STAGE_SLOTS, D, STAGE_COLS), jnp.float32),
        pltpu.SemaphoreType.DMA((STAGE_SLOTS,)),
        pltpu.VMEM((TILE_T, D), bf16),
        pltpu.VMEM((D // LANES, TILE_T + HALO, LANES), jnp.float32),
        pltpu.VMEM((TILE_T, GROUP), jnp.float32),
        pltpu.VMEM((TILE_T, D), bf16),
        pltpu.VMEM((TILE_T, D), bf16),
        pltpu.VMEM((RET_HEADS, RET_DK, RET_DV), jnp.float32),
        pltpu.VMEM((SUBLANES, D), jnp.float32),
    ]
    return pl.pallas_call(
        functools.partial(_block_kernel, final_norm=final_norm),
        grid=(B, S // STEP_T),
        in_specs=in_specs,
        out_specs=tile_spec,
        out_shape=jax.ShapeDtypeStruct(x.shape, x.dtype),
        scratch_shapes=scratch,
        compiler_params=pltpu.CompilerParams(
            dimension_semantics=("arbitrary", "arbitrary"),
            vmem_limit_bytes=VMEM_LIMIT_BYTES),
        name="hybrid_block",
    )(*operands)


def kernel(x, norm_in, w_in, conv_w, conv_b, gate_x_w, gate_x_b, gate_a_w, gate_a_b, lru_lambda, gn_gain, w_proj_a, w_proj_b, w_out, norm_final):
    depth = w_in.shape[0]
    tables = _retention_tables(x.shape[1])
    for l in range(depth):
        x = _layer(x, norm_in[l], w_in[l], conv_w[l], conv_b[l], gate_x_w[l], gate_x_b[l],
                   gate_a_w[l], gate_a_b[l], lru_lambda[l], gn_gain[l],
                   w_proj_a[l], w_proj_b[l], w_out[l], norm_final, tables,
                   final_norm=(l == depth - 1))
    return x
```
